```python
import jax, jax.numpy as jnp
from jax import lax
import numpy as np

D_MODEL = 1024
BATCH = 8
SEQ = 8192
DEPTH = 2

GRID_W = 64
ROPE_THETA = 10000.0
EPS = 1e-6
Q_BLOCK = 128

MLA_HEADS = 8
MLA_Q_LORA = 256
MLA_KV_LORA = 128
MLA_NOPE = 64
MLA_ROPE = 32
MLA_V = 64
MLA_QK = MLA_NOPE + MLA_ROPE

GQA_HEADS = 8
GQA_KV_HEADS = 2
GQA_HEAD_DIM = 64

POOL_WINDOWS = (2, 4, 8, 16)
POOL_GROUPS = 4
POOL_GROUP = 128
POOL_WIDTH = POOL_GROUPS * POOL_GROUP

N_BRANCH = 3
MLA_OUT = MLA_HEADS * MLA_V
GQA_OUT = GQA_HEADS * GQA_HEAD_DIM

D_FF = 2816
CONV_W = 3

IN_SIZES = (MLA_Q_LORA, MLA_KV_LORA, MLA_ROPE,
            GQA_HEADS * GQA_HEAD_DIM, GQA_KV_HEADS * GQA_HEAD_DIM, GQA_KV_HEADS * GQA_HEAD_DIM,
            POOL_WIDTH, N_BRANCH * D_MODEL)
IN_WIDTH = (MLA_Q_LORA + MLA_KV_LORA + MLA_ROPE
            + (GQA_HEADS + 2 * GQA_KV_HEADS) * GQA_HEAD_DIM
            + POOL_WIDTH + N_BRANCH * D_MODEL)

kernel_name = 'hybrid_mla_gqa_pool_convffn_encoder'


def rms_norm(x, g):
    xf = x.astype(jnp.float32)
    y = xf * lax.rsqrt(jnp.mean(xf * xf, axis=-1, keepdims=True) + EPS)
    return (y * g.astype(jnp.float32)).astype(x.dtype)


def split_columns(y, sizes):
    out, start = [], 0
    for n in sizes:
        out.append(y[..., start:start + n])
        start += n
    return out


def axial_rope_tables(seq_len, rot_dim):
    rows = seq_len // GRID_W
    row_idx = jnp.repeat(jnp.arange(rows, dtype=jnp.float32), GRID_W)
    col_idx = jnp.tile(jnp.arange(GRID_W, dtype=jnp.float32), rows)
    n_axis = rot_dim // 4
    inv_freq = ROPE_THETA ** (-jnp.arange(n_axis, dtype=jnp.float32) / n_axis)
    ang = jnp.concatenate([row_idx[:, None] * inv_freq, col_idx[:, None] * inv_freq], axis=-1)
    return jnp.cos(ang), jnp.sin(ang)


def apply_rope(x, cos, sin):
    half = x.shape[-1] // 2
    x1, x2 = x[..., :half], x[..., half:]
    c = cos[None, :, None, :].astype(x.dtype)
    s = sin[None, :, None, :].astype(x.dtype)
    return jnp.concatenate([x1 * c - x2 * s, x1 * s + x2 * c], axis=-1)


def block_attention(q, k, v):
    b, s, h, dk = q.shape
    g = k.shape[2]
    rep = h // g
    dv = v.shape[-1]
    scale = dk ** -0.5
    nb = s // Q_BLOCK
    qb = q.reshape(b, nb, Q_BLOCK, g, rep, dk).transpose(1, 0, 2, 3, 4, 5)

    def one_block(qi):
        sc = jnp.einsum('bqgrd,bkgd->bgrqk', qi, k, preferred_element_type=jnp.float32) * scale
        p = jax.nn.softmax(sc, axis=-1).astype(v.dtype)
        return jnp.einsum('bgrqk,bkgv->bqgrv', p, v)

    out = lax.map(one_block, qb)
    return out.transpose(1, 0, 2, 3, 4, 5).reshape(b, s, h, dv)


def mla_branch(c_q, c_kv, k_rope_raw, q_norm_g, w_uq, kv_norm_g, w_ukv, qh_g, kh_g, cos, sin):
    b, s, _ = c_q.shape
    q = (rms_norm(c_q, q_norm_g) @ w_uq).reshape(b, s, MLA_HEADS, MLA_QK)
    kv = (rms_norm(c_kv, kv_norm_g) @ w_ukv).reshape(b, s, MLA_HEADS, MLA_NOPE + MLA_V)
    k_nope, v = kv[..., :MLA_NOPE], kv[..., MLA_NOPE:]
    k_rope = jnp.broadcast_to(k_rope_raw[:, :, None, :], (b, s, MLA_HEADS, MLA_ROPE))
    k = jnp.concatenate([k_nope, k_rope], axis=-1)
    q = rms_norm(q, qh_g)
    k = rms_norm(k, kh_g)
    q = jnp.concatenate([q[..., :MLA_NOPE], apply_rope(q[..., MLA_NOPE:], cos, sin)], axis=-1)
    k = jnp.concatenate([k[..., :MLA_NOPE], apply_rope(k[..., MLA_NOPE:], cos, sin)], axis=-1)
    o = block_attention(q, k, v)
    return o.reshape(b, s, MLA_OUT)


def gqa_branch(q_raw, k_raw, v_raw, qn_g, kn_g, cos, sin):
    b, s, _ = q_raw.shape
    q = q_raw.reshape(b, s, GQA_HEADS, GQA_HEAD_DIM)
    k = k_raw.reshape(b, s, GQA_KV_HEADS, GQA_HEAD_DIM)
    v = v_raw.reshape(b, s, GQA_KV_HEADS, GQA_HEAD_DIM)
    q = apply_rope(rms_norm(q, qn_g), cos, sin)
    k = apply_rope(rms_norm(k, kn_g), cos, sin)
    o = block_attention(q, k, v)
    return o.reshape(b, s, GQA_OUT)


def pool_branch(p, w_pool, pool_scale):
    b, s, _ = p.shape
    pg = p.reshape(b, s, POOL_GROUPS, POOL_GROUP)
    pf = pg.astype(jnp.float32)
    csum = jnp.concatenate([jnp.zeros((b, 1, POOL_GROUPS, POOL_GROUP), jnp.float32),
                            jnp.cumsum(pf, axis=1)], axis=1)
    t = jnp.arange(s)
    means = []
    for gi, w in enumerate(POOL_WINDOWS):
        lo = jnp.clip(t - w // 2, 0, s)
        hi = jnp.clip(t + w - w // 2, 0, s)
        cg = csum[:, :, gi]
        total = jnp.take(cg, hi, axis=1) - jnp.take(cg, lo, axis=1)
        means.append(total / (hi - lo).astype(jnp.float32)[None, :, None])
    pooled = jnp.stack(means, axis=2)
    mixed = (pooled - pf).astype(p.dtype)
    y = jnp.einsum('bsgc,gcd->bsgd', mixed, w_pool).reshape(b, s, POOL_WIDTH)
    return y * pool_scale


def depthwise_conv(u, w, bias):
    c = u.shape[-1]
    y = lax.conv_general_dilated(u, w[:, None, :], window_strides=(1,),
                                 padding=[(CONV_W // 2, CONV_W // 2)],
                                 dimension_numbers=('NWC', 'WIO', 'NWC'),
                                 feature_group_count=c)
    return y + bias


def setup_inputs(seed: int = 0) -> dict:
    key = jax.random.key(seed)
    ks = jax.random.split(key, 24)
    f32 = jnp.float32

    def dense(k, shape, fan_in):
        return jax.random.normal(k, shape, f32) * (fan_in ** -0.5)

    def gain(k, shape):
        return 1.0 + 0.1 * jax.random.normal(k, shape, f32)

    L = DEPTH
    conv_center = jnp.zeros((CONV_W, 1), f32).at[CONV_W // 2].set(1.0)
    return {
        'x': jax.random.normal(ks[0], (BATCH, SEQ, D_MODEL), f32),
        'attn_norm_g': gain(ks[1], (L, D_MODEL)),
        'w_in': dense(ks[2], (L, D_MODEL, IN_WIDTH), D_MODEL),
        'mla_q_norm_g': gain(ks[3], (L, MLA_Q_LORA)),
        'mla_w_uq': dense(ks[4], (L, MLA_Q_LORA, MLA_HEADS * MLA_QK), MLA_Q_LORA),
        'mla_kv_norm_g': gain(ks[5], (L, MLA_KV_LORA)),
        'mla_w_ukv': dense(ks[6], (L, MLA_KV_LORA, MLA_HEADS * (MLA_NOPE + MLA_V)), MLA_KV_LORA),
        'mla_q_head_g': gain(ks[7], (L, MLA_QK)),
        'mla_k_head_g': gain(ks[8], (L, MLA_QK)),
        'gqa_q_head_g': gain(ks[9], (L, GQA_HEAD_DIM)),
        'gqa_k_head_g': gain(ks[10], (L, GQA_HEAD_DIM)),
        'pool_w': dense(ks[11], (L, POOL_GROUPS, POOL_GROUP, POOL_GROUP), POOL_GROUP),
        'pool_scale': gain(ks[12], (L, POOL_WIDTH)),
        'w_branch_mla': dense(ks[13], (L, MLA_OUT, D_MODEL), MLA_OUT),
        'w_branch_gqa': dense(ks[14], (L, GQA_OUT, D_MODEL), GQA_OUT),
        'w_branch_pool': dense(ks[15], (L, POOL_WIDTH, D_MODEL), POOL_WIDTH),
        'w_out': dense(ks[16], (L, D_MODEL, D_MODEL), D_MODEL),
        'ffn_norm_g': gain(ks[17], (L, D_MODEL)),
        'ffn_w_up': dense(ks[18], (L, D_MODEL, 2 * D_FF), D_MODEL),
        'ffn_conv_w': conv_center[None] + 0.3 * jax.random.normal(ks[19], (L, CONV_W, 2 * D_FF), f32),
        'ffn_conv_b': 0.01 * jax.random.normal(ks[20], (L, 2 * D_FF), f32),
        'ffn_w_down': dense(ks[21], (L, D_FF, D_MODEL), D_FF),
    }


def reference(x, attn_norm_g, w_in, mla_q_norm_g, mla_w_uq, mla_kv_norm_g, mla_w_ukv,
              mla_q_head_g, mla_k_head_g, gqa_q_head_g, gqa_k_head_g, pool_w, pool_scale,
              w_branch_mla, w_branch_gqa, w_branch_pool, w_out, ffn_norm_g, ffn_w_up,
              ffn_conv_w, ffn_conv_b, ffn_w_down):
    b, s, d = x.shape
    cos_m, sin_m = axial_rope_tables(s, MLA_ROPE)
    cos_g, sin_g = axial_rope_tables(s, GQA_HEAD_DIM)
    for l in range(DEPTH):
        h = rms_norm(x, attn_norm_g[l])
        proj = h @ w_in[l]
        c_q, c_kv, k_rope, gq, gk, gv, pool_in, gate_logits = split_columns(proj, IN_SIZES)
        o_mla = mla_branch(c_q, c_kv, k_rope, mla_q_norm_g[l], mla_w_uq[l], mla_kv_norm_g[l],
                           mla_w_ukv[l], mla_q_head_g[l], mla_k_head_g[l], cos_m, sin_m)
        o_gqa = gqa_branch(gq, gk, gv, gqa_q_head_g[l], gqa_k_head_g[l], cos_g, sin_g)
        o_pool = pool_branch(pool_in, pool_w[l], pool_scale[l])
        gates = jax.nn.sigmoid(gate_logits.astype(jnp.float32)).astype(x.dtype).reshape(b, s, N_BRANCH, d)
        merged = (gates[:, :, 0] * (o_mla @ w_branch_mla[l])
                  + gates[:, :, 1] * (o_gqa @ w_branch_gqa[l])
                  + gates[:, :, 2] * (o_pool @ w_branch_pool[l]))
        x = x + merged @ w_out[l]
        hf = rms_norm(x, ffn_norm_g[l])
        u = depthwise_conv(hf @ ffn_w_up[l], ffn_conv_w[l], ffn_conv_b[l])
        u_gate, u_val = u[..., :D_FF], u[..., D_FF:]
        x = x + (jax.nn.silu(u_gate) * u_val) @ ffn_w_down[l]
    return x
```

```python
import functools
import math

import numpy as np
import jax
import jax.numpy as jnp
from jax import lax
from jax.experimental import pallas as pl
from jax.experimental.pallas import tpu as pltpu

F32 = jnp.float32
BF16 = jnp.bfloat16

D_MODEL = 1024
GRID_W = 64
ROPE_THETA = 10000.0
EPS = 1e-6
MLA_HEADS = 8
MLA_Q_LORA = 256
MLA_KV_LORA = 128
MLA_NOPE = 64
MLA_ROPE = 32
MLA_V = 64
MLA_QK = MLA_NOPE + MLA_ROPE
GQA_HEADS = 8
GQA_KV_HEADS = 2
GQA_HEAD_DIM = 64
POOL_WINDOWS = (2, 4, 8, 16)
POOL_GROUPS = 4
POOL_GROUP = 128
POOL_WIDTH = POOL_GROUPS * POOL_GROUP
N_BRANCH = 3
D_FF = 2816

LANES = 128
ROPE_LANE_SHIFT = 64
HALO = 16
VMEM_LIMIT = 56 * 1024 * 1024

TOKEN_TILE = 512
Q_TILE_MLA = 512
Q_TILE_GQA = 256
KV_CHUNK = 512
FF_CHUNK = 256
N_FF_CHUNKS = D_FF // FF_CHUNK

_OFF_CQ = 0
_OFF_CKV = _OFF_CQ + MLA_Q_LORA
_OFF_KROPE = _OFF_CKV + MLA_KV_LORA
_OFF_GQ = _OFF_KROPE + MLA_ROPE
_OFF_GK = _OFF_GQ + GQA_HEADS * GQA_HEAD_DIM
_OFF_GV = _OFF_GK + GQA_KV_HEADS * GQA_HEAD_DIM
_OFF_POOL = _OFF_GV + GQA_KV_HEADS * GQA_HEAD_DIM
_OFF_GATE = _OFF_POOL + POOL_WIDTH


def _mla_head_perm():
    perm = np.full((LANES,), -1, np.int64)
    half = MLA_ROPE // 2
    perm[0:half] = MLA_NOPE + np.arange(half)
    perm[ROPE_LANE_SHIFT:ROPE_LANE_SHIFT + half] = MLA_NOPE + half + np.arange(half)
    perm[half:ROPE_LANE_SHIFT] = np.arange(ROPE_LANE_SHIFT - half)
    rest = MLA_NOPE - (ROPE_LANE_SHIFT - half)
    perm[ROPE_LANE_SHIFT + half:ROPE_LANE_SHIFT + half + rest] = (ROPE_LANE_SHIFT - half) + np.arange(rest)
    return perm


def _gqa_head_perm():
    perm = np.full((LANES,), -1, np.int64)
    half = GQA_HEAD_DIM // 2
    perm[0:half] = np.arange(half)
    perm[ROPE_LANE_SHIFT:ROPE_LANE_SHIFT + half] = half + np.arange(half)
    return perm


def _take_cols(w, idx):
    idx = np.asarray(idx)
    cols = jnp.take(w, jnp.asarray(np.maximum(idx, 0)), axis=-1)
    return jnp.where(jnp.asarray(idx >= 0), cols, jnp.zeros((), w.dtype))


def _prep_in_cols():
    mla = _mla_head_perm()
    gqa = _gqa_head_perm()
    cols = [np.arange(_OFF_CQ, _OFF_CQ + MLA_Q_LORA), np.arange(_OFF_CKV, _OFF_CKV + MLA_KV_LORA)]
    kr = np.where(mla >= MLA_NOPE, _OFF_KROPE + (mla - MLA_NOPE), -1)
    cols.append(kr)
    for h in range(GQA_HEADS):
        cols.append(np.where(gqa >= 0, _OFF_GQ + h * GQA_HEAD_DIM + gqa, -1))
    for g in range(GQA_KV_HEADS):
        cols.append(np.where(gqa >= 0, _OFF_GK + g * GQA_HEAD_DIM + gqa, -1))
    for g in range(GQA_KV_HEADS):
        v = np.full((LANES,), -1, np.int64)
        v[:GQA_HEAD_DIM] = _OFF_GV + g * GQA_HEAD_DIM + np.arange(GQA_HEAD_DIM)
        cols.append(v)
    return np.concatenate(cols)


_PREP_W = MLA_Q_LORA + MLA_KV_LORA + LANES + (GQA_HEADS + 2 * GQA_KV_HEADS) * LANES
_P_CQ = 0
_P_CKV = MLA_Q_LORA
_P_KROPE = _P_CKV + MLA_KV_LORA
_P_GQ = _P_KROPE + LANES
_P_GK = _P_GQ + GQA_HEADS * LANES
_P_GV = _P_GK + GQA_KV_HEADS * LANES


def _uq_cols():
    mla = _mla_head_perm()
    return np.concatenate([np.where(mla >= 0, h * MLA_QK + mla, -1) for h in range(MLA_HEADS)])


def _ukv_cols():
    mla = _mla_head_perm()
    per_head = MLA_NOPE + MLA_V
    k = [np.where((mla >= 0) & (mla < MLA_NOPE), h * per_head + mla, -1) for h in range(MLA_HEADS)]
    v = [h * per_head + MLA_NOPE + np.arange(MLA_V) for h in range(MLA_HEADS)]
    return np.concatenate(k + v)


def _rope_tables(seq_len, rot_dim, perm, first_rope_dim):
    rows = seq_len // GRID_W
    row_idx = jnp.repeat(jnp.arange(rows, dtype=F32), GRID_W)
    col_idx = jnp.tile(jnp.arange(GRID_W, dtype=F32), rows)
    n_axis = rot_dim // 4
    inv_freq = ROPE_THETA ** (-jnp.arange(n_axis, dtype=F32) / n_axis)
    ang = jnp.concatenate([row_idx[:, None] * inv_freq, col_idx[:, None] * inv_freq], axis=-1)
    cos, sin = jnp.cos(ang), jnp.sin(ang)
    half = rot_dim // 2
    rel = perm - first_rope_dim
    is_rope = (perm >= first_rope_dim)
    freq = np.where(is_rope, rel % half, 0)
    sign = np.where(is_rope, np.where(rel < half, -1.0, 1.0), 0.0).astype(np.float32)
    c = jnp.where(jnp.asarray(is_rope), jnp.take(cos, jnp.asarray(freq), axis=1), 1.0)
    s = jnp.take(sin, jnp.asarray(freq), axis=1) * jnp.asarray(sign)
    return c.astype(F32), s.astype(F32)


def _rms(x, g):
    return x * lax.rsqrt(jnp.mean(x * x, axis=-1, keepdims=True) + EPS) * g


def _head_norm_rope(xh, g, cos, sin, dim):
    ms = jnp.sum(xh * xh, axis=-1, keepdims=True) * (1.0 / dim)
    xn = xh * lax.rsqrt(ms + EPS) * g
    return xn * cos + pltpu.roll(xn, ROPE_LANE_SHIFT, 1) * sin


def _dot(a, b):
    return jnp.dot(a, b, preferred_element_type=F32)


def _prep_body(x_ref, g_ref, win_ref, qng_ref, wuq_ref, kvng_ref, wukv_ref, qhg_ref, khg_ref,
               gqg_ref, gkg_ref, cm_ref, sm_ref, cg_ref, sg_ref,
               qm_ref, km_ref, vm_ref, qg_ref, kg_ref, vg_ref):
    h = _rms(x_ref[0], g_ref[...]).astype(BF16)
    proj = _dot(h, win_ref[...])
    cq = _rms(proj[:, _P_CQ:_P_CQ + MLA_Q_LORA], qng_ref[...]).astype(BF16)
    ckv = _rms(proj[:, _P_CKV:_P_CKV + MLA_KV_LORA], kvng_ref[...]).astype(BF16)
    krope = proj[:, _P_KROPE:_P_KROPE + LANES]
    q = _dot(cq, wuq_ref[...])
    kv = _dot(ckv, wukv_ref[...])
    cm, sm, cg, sg = cm_ref[...], sm_ref[...], cg_ref[...], sg_ref[...]
    qhg, khg, gqg, gkg = qhg_ref[...], khg_ref[...], gqg_ref[...], gkg_ref[...]
    for hd in range(MLA_HEADS):
        sl = slice(hd * LANES, (hd + 1) * LANES)
        qm_ref[0, :, sl] = _head_norm_rope(q[:, sl], qhg, cm, sm, MLA_QK).astype(BF16)
        km_ref[0, :, sl] = _head_norm_rope(kv[:, sl] + krope, khg, cm, sm, MLA_QK).astype(BF16)
    vm_ref[0] = kv[:, MLA_HEADS * LANES:].astype(BF16)
    for hd in range(GQA_HEADS):
        sl = slice(hd * LANES, (hd + 1) * LANES)
        src = proj[:, _P_GQ + hd * LANES:_P_GQ + (hd + 1) * LANES]
        qg_ref[0, :, sl] = _head_norm_rope(src, gqg, cg, sg, GQA_HEAD_DIM).astype(BF16)
    for g in range(GQA_KV_HEADS):
        sl = slice(g * LANES, (g + 1) * LANES)
        src = proj[:, _P_GK + g * LANES:_P_GK + (g + 1) * LANES]
        kg_ref[0, :, sl] = _head_norm_rope(src, gkg, cg, sg, GQA_HEAD_DIM).astype(BF16)
    vg_ref[0] = proj[:, _P_GV:_P_GV + GQA_KV_HEADS * LANES].astype(BF16)


def _const_spec(shape):
    nd = len(shape)
    return pl.BlockSpec(shape, lambda *_: (0,) * nd)


def _prep(x, g, win, qng, wuq, kvng, wukv, qhg, khg, gqg, gkg, cm, sm, cg, sg):
    b, s, d = x.shape
    t = TOKEN_TILE
    tok = lambda w: pl.BlockSpec((1, t, w), lambda bi, i: (bi, i, 0))
    tab = pl.BlockSpec((t, LANES), lambda bi, i: (i, 0))
    consts = (g, win, qng, wuq, kvng, wukv, qhg, khg, gqg, gkg)
    out_w = (MLA_HEADS * LANES, MLA_HEADS * LANES, MLA_HEADS * MLA_V,
             GQA_HEADS * LANES, GQA_KV_HEADS * LANES, GQA_KV_HEADS * LANES)
    return pl.pallas_call(
        _prep_body,
        grid=(b, s // t),
        in_specs=[tok(d)] + [_const_spec(c.shape) for c in consts] + [tab] * 4,
        out_specs=[tok(w) for w in out_w],
        out_shape=[jax.ShapeDtypeStruct((b, s, w), BF16) for w in out_w],
        compiler_params=pltpu.CompilerParams(
            dimension_semantics=("parallel", "parallel"), vmem_limit_bytes=VMEM_LIMIT),
        name="prep",
    )(x, *consts, cm, sm, cg, sg)


def _flash(q, k_ref, k_off, v_ref, v_off, seq_len, exp2_scale):
    rows = q.shape[0]
    nt_dims = (((1,), (1,)), ((), ()))

    def step(j, carry):
        m, l, acc = carry
        start = pl.multiple_of(j * KV_CHUNK, KV_CHUNK)
        kc = k_ref[0, pl.ds(start, KV_CHUNK), k_off:k_off + LANES]
        vc = v_ref[0, pl.ds(start, KV_CHUNK), v_off:v_off + LANES]
        s = lax.dot_general(q, kc, nt_dims, preferred_element_type=F32)
        m_new = jnp.maximum(m, jnp.max(s, axis=1, keepdims=True))
        alpha = jnp.exp2((m - m_new) * exp2_scale)
        p = jnp.exp2((s - m_new) * exp2_scale)
        l_new = alpha * l + jnp.sum(p, axis=1, keepdims=True)
        acc_new = alpha * acc + _dot(p.astype(BF16), vc)
        return m_new, l_new, acc_new

    init = (jnp.full((rows, 1), -1e30, F32), jnp.zeros((rows, 1), F32), jnp.zeros((rows, LANES), F32))
    _, l, acc = lax.fori_loop(0, seq_len // KV_CHUNK, step, init)
    return acc / l


def _attn_mla_body(q_ref, k_ref, v_ref, o_ref, *, seq_len, exp2_scale):
    o0 = _flash(q_ref[0, :, 0:LANES], k_ref, 0, v_ref, 0, seq_len, exp2_scale)
    o1 = _flash(q_ref[0, :, LANES:2 * LANES], k_ref, LANES, v_ref, 0, seq_len, exp2_scale)
    lane = lax.broadcasted_iota(jnp.int32, o0.shape, 1)
    o_ref[0] = jnp.where(lane < MLA_V, o0, o1).astype(o_ref.dtype)


def _attn_gqa_body(q_ref, k_ref, v_ref, o_ref, *, seq_len, exp2_scale):
    rep = GQA_HEADS // GQA_KV_HEADS
    tq = q_ref.shape[1]
    q = jnp.concatenate([q_ref[0, :, i * LANES:(i + 1) * LANES] for i in range(rep)], axis=0)
    o = _flash(q, k_ref, 0, v_ref, 0, seq_len, exp2_scale)
    for j in range(rep // 2):
        even = o[(2 * j) * tq:(2 * j + 1) * tq]
        odd = o[(2 * j + 1) * tq:(2 * j + 2) * tq]
        o_ref[0, :, j * LANES:(j + 1) * LANES] = (even + pltpu.roll(odd, GQA_HEAD_DIM, 1)).astype(o_ref.dtype)


def _attn_mla(q, k, v):
    b, s, _ = q.shape
    tq = Q_TILE_MLA
    body = functools.partial(_attn_mla_body, seq_len=s, exp2_scale=MLA_QK ** -0.5 * math.log2(math.e))
    return pl.pallas_call(
        body,
        grid=(b, MLA_HEADS // 2, s // tq),
        in_specs=[pl.BlockSpec((1, tq, 2 * LANES), lambda bi, hp, qi: (bi, qi, hp)),
                  pl.BlockSpec((1, s, 2 * LANES), lambda bi, hp, qi: (bi, 0, hp)),
                  pl.BlockSpec((1, s, LANES), lambda bi, hp, qi: (bi, 0, hp))],
        out_specs=pl.BlockSpec((1, tq, LANES), lambda bi, hp, qi: (bi, qi, hp)),
        out_shape=jax.ShapeDtypeStruct((b, s, MLA_HEADS * MLA_V), BF16),
        compiler_params=pltpu.CompilerParams(
            dimension_semantics=("parallel", "parallel", "arbitrary"), vmem_limit_bytes=VMEM_LIMIT),
        name="attn_mla",
    )(q, k, v)


def _attn_gqa(q, k, v):
    b, s, _ = q.shape
    tq = Q_TILE_GQA
    rep = GQA_HEADS // GQA_KV_HEADS
    body = functools.partial(_attn_gqa_body, seq_len=s, exp2_scale=GQA_HEAD_DIM ** -0.5 * math.log2(math.e))
    return pl.pallas_call(
        body,
        grid=(b, GQA_KV_HEADS, s // tq),
        in_specs=[pl.BlockSpec((1, tq, rep * LANES), lambda bi, g, qi: (bi, qi, g)),
                  pl.BlockSpec((1, s, LANES), lambda bi, g, qi: (bi, 0, g)),
                  pl.BlockSpec((1, s, LANES), lambda bi, g, qi: (bi, 0, g))],
        out_specs=pl.BlockSpec((1, tq, rep * GQA_HEAD_DIM), lambda bi, g, qi: (bi, qi, g)),
        out_shape=jax.ShapeDtypeStruct((b, s, GQA_HEADS * GQA_HEAD_DIM), BF16),
        compiler_params=pltpu.CompilerParams(
            dimension_semantics=("parallel", "parallel", "arbitrary"), vmem_limit_bytes=VMEM_LIMIT),
        name="attn_gqa",
    )(q, k, v)


def _fill_normed_ext(h_scr, x, xp_ref, xn_ref, g):
    i = pl.program_id(1)
    t = x.shape[0]
    h_scr[HALO:HALO + t] = _rms(x, g).astype(BF16)
    hp = _rms(xp_ref[0], g)
    hn = _rms(xn_ref[0], g)
    h_scr[0:HALO] = jnp.where(i > 0, hp, 0.0).astype(BF16)
    h_scr[HALO + t:] = jnp.where(i < pl.num_programs(1) - 1, hn, 0.0).astype(BF16)


def _halo_specs(t, d):
    per = t // HALO
    main = pl.BlockSpec((1, t, d), lambda bi, i: (bi, i, 0))
    prev = pl.BlockSpec((1, HALO, d), lambda bi, i: (bi, jnp.maximum(i * per - 1, 0), 0))
    nxt = lambda n_blocks: pl.BlockSpec(
        (1, HALO, d), lambda bi, i: (bi, jnp.minimum((i + 1) * per, n_blocks - 1), 0))
    return main, prev, nxt


def _merge_body(x_ref, xp_ref, xn_ref, om_ref, og_ref, g_ref, wgate_ref, wpool_ref, poolw_ref, pscale_ref,
                wbm_ref, wbg_ref, wbp_ref, wout_ref, o_ref, h_scr, p_scr, *, seq_len):
    x = x_ref[0]
    t = x.shape[0]
    _fill_normed_ext(h_scr, x, xp_ref, xn_ref, g_ref[...])
    h = h_scr[HALO:HALO + t]
    p_scr[...] = _dot(h_scr[...], wpool_ref[...])
    pos = pl.program_id(1) * t + lax.broadcasted_iota(jnp.int32, (t, 1), 0)
    ys = []
    for gi, w in enumerate(POOL_WINDOWS):
        sl = slice(gi * POOL_GROUP, (gi + 1) * POOL_GROUP)
        tot = None
        for d in range(-(w // 2), w - w // 2):
            part = p_scr[HALO + d:HALO + d + t, sl]
            tot = part if tot is None else tot + part
        cnt = jnp.minimum(pos + (w - w // 2), seq_len) - jnp.maximum(pos - w // 2, 0)
        mixed = tot / cnt.astype(F32) - p_scr[HALO:HALO + t, sl]
        ys.append(_dot(mixed.astype(BF16), poolw_ref[gi]))
    y = (jnp.concatenate(ys, axis=1) * pscale_ref[...]).astype(BF16)
    gates = jax.nn.sigmoid(_dot(h, wgate_ref[...]))
    merged = (gates[:, 0:D_MODEL] * _dot(om_ref[0], wbm_ref[...])
              + gates[:, D_MODEL:2 * D_MODEL] * _dot(og_ref[0], wbg_ref[...])
              + gates[:, 2 * D_MODEL:3 * D_MODEL] * _dot(y, wbp_ref[...]))
    o_ref[0] = x + _dot(merged.astype(BF16), wout_ref[...])


def _merge(x, om, og, g, wgate, wpool, poolw, pscale, wbm, wbg, wbp, wout):
    b, s, d = x.shape
    t = TOKEN_TILE
    main, prev, nxt = _halo_specs(t, d)
    tok = lambda w: pl.BlockSpec((1, t, w), lambda bi, i: (bi, i, 0))
    consts = (g, wgate, wpool, poolw, pscale, wbm, wbg, wbp, wout)
    return pl.pallas_call(
        functools.partial(_merge_body, seq_len=s),
        grid=(b, s // t),
        in_specs=[main, prev, nxt(s // HALO), tok(om.shape[-1]), tok(og.shape[-1])]
                 + [_const_spec(c.shape) for c in consts],
        out_specs=main,
        out_shape=jax.ShapeDtypeStruct(x.shape, x.dtype),
        scratch_shapes=[pltpu.VMEM((t + 2 * HALO, d), BF16), pltpu.VMEM((t + 2 * HALO, POOL_WIDTH), F32)],
        compiler_params=pltpu.CompilerParams(
            dimension_semantics=("parallel", "parallel"), vmem_limit_bytes=VMEM_LIMIT),
        name="merge",
    )(x, x, x, om, og, *consts)


def _ffn_body(x_ref, xp_ref, xn_ref, g_ref, wup_ref, cw_ref, wdn_ref, o_ref, h_scr, u_scr, acc_scr):
    x = x_ref[0]
    t = x.shape[0]
    _fill_normed_ext(h_scr, x, xp_ref, xn_ref, g_ref[...])
    acc_scr[...] = jnp.zeros_like(acc_scr)

    def conv_half(k):
        u_scr[...] = _dot(h_scr[...], wup_ref[k])
        cw = cw_ref[k]
        return (cw[0:1] * u_scr[HALO - 1:HALO - 1 + t] + cw[1:2] * u_scr[HALO:HALO + t]
                + cw[2:3] * u_scr[HALO + 1:HALO + 1 + t] + cw[3:4])

    def chunk(c, carry):
        u_gate = conv_half(c)
        u_val = conv_half(c + N_FF_CHUNKS)
        act = (jax.nn.silu(u_gate) * u_val).astype(BF16)
        acc_scr[...] += _dot(act, wdn_ref[c])
        return carry

    lax.fori_loop(0, N_FF_CHUNKS, chunk, 0)
    o_ref[0] = x + acc_scr[...]


def _ffn(x, g, wup, cw, wdn):
    b, s, d = x.shape
    t = TOKEN_TILE
    main, prev, nxt = _halo_specs(t, d)
    consts = (g, wup, cw, wdn)
    return pl.pallas_call(
        _ffn_body,
        grid=(b, s // t),
        in_specs=[main, prev, nxt(s // HALO)] + [_const_spec(c.shape) for c in consts],
        out_specs=main,
        out_shape=jax.ShapeDtypeStruct(x.shape, x.dtype),
        scratch_shapes=[pltpu.VMEM((t + 2 * HALO, d), BF16), pltpu.VMEM((t + 2 * HALO, FF_CHUNK), F32),
                        pltpu.VMEM((t, d), F32)],
        compiler_params=pltpu.CompilerParams(
            dimension_semantics=("parallel", "parallel"), vmem_limit_bytes=VMEM_LIMIT),
        name="ffn",
    )(x, x, x, *consts)


def _pad_gain(g, perm):
    return _take_cols(g[None, :], perm)


def kernel(x, attn_norm_g, w_in, mla_q_norm_g, mla_w_uq, mla_kv_norm_g, mla_w_ukv, mla_q_head_g, mla_k_head_g, gqa_q_head_g, gqa_k_head_g, pool_w, pool_scale, w_branch_mla, w_branch_gqa, w_branch_pool, w_out, ffn_norm_g, ffn_w_up, ffn_conv_w, ffn_conv_b, ffn_w_down):
    b, s, d = x.shape
    depth = w_in.shape[0]
    mla_perm, gqa_perm = _mla_head_perm(), _gqa_head_perm()
    cm, sm = _rope_tables(s, MLA_ROPE, mla_perm, MLA_NOPE)
    cg, sg = _rope_tables(s, GQA_HEAD_DIM, gqa_perm, 0)
    prep_cols, uq_cols, ukv_cols = _prep_in_cols(), _uq_cols(), _ukv_cols()
    row = lambda v: v[None, :].astype(F32)
    for l in range(depth):
        qm, km, vm, qg, kg, vg = _prep(
            x, row(attn_norm_g[l]), _take_cols(w_in[l], prep_cols).astype(BF16),
            row(mla_q_norm_g[l]), _take_cols(mla_w_uq[l], uq_cols).astype(BF16),
            row(mla_kv_norm_g[l]), _take_cols(mla_w_ukv[l], ukv_cols).astype(BF16),
            _pad_gain(mla_q_head_g[l], mla_perm), _pad_gain(mla_k_head_g[l], mla_perm),
            _pad_gain(gqa_q_head_g[l], gqa_perm), _pad_gain(gqa_k_head_g[l], gqa_perm),
            cm, sm, cg, sg)
        o_mla = _attn_mla(qm, km, vm)
        o_gqa = _attn_gqa(qg, kg, vg)
        x = _merge(
            x, o_mla, o_gqa, row(attn_norm_g[l]),
            w_in[l][:, _OFF_GATE:].astype(BF16), w_in[l][:, _OFF_POOL:_OFF_GATE].astype(BF16),
            pool_w[l].astype(BF16), row(pool_scale[l]),
            w_branch_mla[l].astype(BF16), w_branch_gqa[l].astype(BF16), w_branch_pool[l].astype(BF16),
            w_out[l].astype(BF16))
        wup = ffn_w_up[l].astype(BF16).reshape(d, 2 * N_FF_CHUNKS, FF_CHUNK).transpose(1, 0, 2)
        cw = jnp.concatenate([ffn_conv_w[l], ffn_conv_b[l][None, :], jnp.zeros((4, 2 * D_FF), F32)], axis=0)
        cw = cw.reshape(8, 2 * N_FF_CHUNKS, FF_CHUNK).transpose(1, 0, 2)
        wdn = ffn_w_down[l].astype(BF16).reshape(N_FF_CHUNKS, FF_CHUNK, d)
        x = _ffn(x, row(ffn_norm_g[l]), wup, cw, wdn)
    return x
```

```python
import functools
import math

import numpy as np
import jax
import jax.numpy as jnp
from jax import lax
from jax.experimental import pallas as pl
from jax.experimental.pallas import tpu as pltpu

F32 = jnp.float32
BF16 = jnp.bfloat16

D_MODEL = 1024
GRID_W = 64
ROPE_THETA = 10000.0
EPS = 1e-6
MLA_HEADS = 8
MLA_Q_LORA = 256
MLA_KV_LORA = 128
MLA_NOPE = 64
MLA_ROPE = 32
MLA_V = 64
MLA_QK = MLA_NOPE + MLA_ROPE
GQA_HEADS = 8
GQA_KV_HEADS = 2
GQA_HEAD_DIM = 64
POOL_WINDOWS = (2, 4, 8, 16)
POOL_GROUPS = 4
POOL_GROUP = 128
POOL_WIDTH = POOL_GROUPS * POOL_GROUP
N_BRANCH = 3
D_FF = 2816

LANES = 128
ROPE_LANE_SHIFT = 64
HALO = 16
HEAD_V = 64
DENOM_LANE = HEAD_V
VMEM_LIMIT = 56 * 1024 * 1024

TOKEN_TILE = 512
Q_TILE_MLA = 1024
Q_TILE_GQA = 256
KV_CHUNK = 1024
FF_CHUNK = 256
N_FF_CHUNKS = D_FF // FF_CHUNK

MAX_SHIFT_FREE_SCORE = 60.0

_OFF_CQ = 0
_OFF_CKV = _OFF_CQ + MLA_Q_LORA
_OFF_KROPE = _OFF_CKV + MLA_KV_LORA
_OFF_GQ = _OFF_KROPE + MLA_ROPE
_OFF_GK = _OFF_GQ + GQA_HEADS * GQA_HEAD_DIM
_OFF_GV = _OFF_GK + GQA_KV_HEADS * GQA_HEAD_DIM
_OFF_POOL = _OFF_GV + GQA_KV_HEADS * GQA_HEAD_DIM
_OFF_GATE = _OFF_POOL + POOL_WIDTH


def _mla_head_perm():
    perm = np.full((LANES,), -1, np.int64)
    half = MLA_ROPE // 2
    perm[0:half] = MLA_NOPE + np.arange(half)
    perm[ROPE_LANE_SHIFT:ROPE_LANE_SHIFT + half] = MLA_NOPE + half + np.arange(half)
    perm[half:ROPE_LANE_SHIFT] = np.arange(ROPE_LANE_SHIFT - half)
    rest = MLA_NOPE - (ROPE_LANE_SHIFT - half)
    perm[ROPE_LANE_SHIFT + half:ROPE_LANE_SHIFT + half + rest] = (ROPE_LANE_SHIFT - half) + np.arange(rest)
    return perm


def _gqa_head_perm():
    perm = np.full((LANES,), -1, np.int64)
    half = GQA_HEAD_DIM // 2
    perm[0:half] = np.arange(half)
    perm[ROPE_LANE_SHIFT:ROPE_LANE_SHIFT + half] = half + np.arange(half)
    return perm


def _take_cols(w, idx):
    idx = np.asarray(idx)
    cols = jnp.take(w, jnp.asarray(np.maximum(idx, 0)), axis=-1)
    return jnp.where(jnp.asarray(idx >= 0), cols, jnp.zeros((), w.dtype))


def _prep_in_cols():
    mla = _mla_head_perm()
    gqa = _gqa_head_perm()
    cols = [np.arange(_OFF_CQ, _OFF_CQ + MLA_Q_LORA), np.arange(_OFF_CKV, _OFF_CKV + MLA_KV_LORA)]
    kr = np.where(mla >= MLA_NOPE, _OFF_KROPE + (mla - MLA_NOPE), -1)
    cols.append(kr)
    for h in range(GQA_HEADS):
        cols.append(np.where(gqa >= 0, _OFF_GQ + h * GQA_HEAD_DIM + gqa, -1))
    for g in range(GQA_KV_HEADS):
        cols.append(np.where(gqa >= 0, _OFF_GK + g * GQA_HEAD_DIM + gqa, -1))
    for g in range(GQA_KV_HEADS):
        v = np.full((LANES,), -1, np.int64)
        v[:GQA_HEAD_DIM] = _OFF_GV + g * GQA_HEAD_DIM + np.arange(GQA_HEAD_DIM)
        cols.append(v)
    return np.concatenate(cols)


_PREP_W = MLA_Q_LORA + MLA_KV_LORA + LANES + (GQA_HEADS + 2 * GQA_KV_HEADS) * LANES
_P_CQ = 0
_P_CKV = MLA_Q_LORA
_P_KROPE = _P_CKV + MLA_KV_LORA
_P_GQ = _P_KROPE + LANES
_P_GK = _P_GQ + GQA_HEADS * LANES
_P_GV = _P_GK + GQA_KV_HEADS * LANES


def _uq_cols():
    mla = _mla_head_perm()
    return np.concatenate([np.where(mla >= 0, h * MLA_QK + mla, -1) for h in range(MLA_HEADS)])


def _ukv_cols():
    mla = _mla_head_perm()
    per_head = MLA_NOPE + MLA_V
    k = [np.where((mla >= 0) & (mla < MLA_NOPE), h * per_head + mla, -1) for h in range(MLA_HEADS)]
    v = []
    for h in range(MLA_HEADS):
        vh = np.full((LANES,), -1, np.int64)
        vh[:MLA_V] = h * per_head + MLA_NOPE + np.arange(MLA_V)
        v.append(vh)
    return np.concatenate(k + v)


def _rope_tables(seq_len, rot_dim, perm, first_rope_dim):
    rows = seq_len // GRID_W
    row_idx = jnp.repeat(jnp.arange(rows, dtype=F32), GRID_W)
    col_idx = jnp.tile(jnp.arange(GRID_W, dtype=F32), rows)
    n_axis = rot_dim // 4
    inv_freq = ROPE_THETA ** (-jnp.arange(n_axis, dtype=F32) / n_axis)
    ang = jnp.concatenate([row_idx[:, None] * inv_freq, col_idx[:, None] * inv_freq], axis=-1)
    cos, sin = jnp.cos(ang), jnp.sin(ang)
    half = rot_dim // 2
    rel = perm - first_rope_dim
    is_rope = (perm >= first_rope_dim)
    freq = np.where(is_rope, rel % half, 0)
    sign = np.where(is_rope, np.where(rel < half, -1.0, 1.0), 0.0).astype(np.float32)
    c = jnp.where(jnp.asarray(is_rope), jnp.take(cos, jnp.asarray(freq), axis=1), 1.0)
    s = jnp.take(sin, jnp.asarray(freq), axis=1) * jnp.asarray(sign)
    return c.astype(F32), s.astype(F32)


def _rms(x, g):
    return x * lax.rsqrt(jnp.mean(x * x, axis=-1, keepdims=True) + EPS) * g


def _head_norm_rope(xh, g, cos, sin, dim):
    ms = jnp.sum(xh * xh, axis=-1, keepdims=True) * (1.0 / dim)
    xn = xh * lax.rsqrt(ms + EPS) * g
    return xn * cos + pltpu.roll(xn, ROPE_LANE_SHIFT, 1) * sin


def _dot(a, b):
    return jnp.dot(a, b, preferred_element_type=F32)


def _with_denominator_column(v):
    lane = lax.broadcasted_iota(jnp.int32, v.shape, 1)
    return jnp.where((lane & (LANES - 1)) == DENOM_LANE, 1.0, v)


def _prep_body(x_ref, g_ref, win_ref, qng_ref, wuq_ref, kvng_ref, wukv_ref, qhg_ref, khg_ref,
               gqg_ref, gkg_ref, cm_ref, sm_ref, cg_ref, sg_ref,
               qm_ref, km_ref, vm_ref, qg_ref, kg_ref, vg_ref):
    h = _rms(x_ref[0], g_ref[...]).astype(BF16)
    proj = _dot(h, win_ref[...])
    cq = _rms(proj[:, _P_CQ:_P_CQ + MLA_Q_LORA], qng_ref[...]).astype(BF16)
    ckv = _rms(proj[:, _P_CKV:_P_CKV + MLA_KV_LORA], kvng_ref[...]).astype(BF16)
    krope = proj[:, _P_KROPE:_P_KROPE + LANES]
    q = _dot(cq, wuq_ref[...])
    kv = _dot(ckv, wukv_ref[...])
    cm, sm, cg, sg = cm_ref[...], sm_ref[...], cg_ref[...], sg_ref[...]
    qhg, khg, gqg, gkg = qhg_ref[...], khg_ref[...], gqg_ref[...], gkg_ref[...]
    for hd in range(MLA_HEADS):
        sl = slice(hd * LANES, (hd + 1) * LANES)
        qm_ref[0, :, sl] = _head_norm_rope(q[:, sl], qhg, cm, sm, MLA_QK).astype(BF16)
        km_ref[0, :, sl] = _head_norm_rope(kv[:, sl] + krope, khg, cm, sm, MLA_QK).astype(BF16)
    vm_ref[0] = _with_denominator_column(kv[:, MLA_HEADS * LANES:]).astype(BF16)
    for hd in range(GQA_HEADS):
        sl = slice(hd * LANES, (hd + 1) * LANES)
        src = proj[:, _P_GQ + hd * LANES:_P_GQ + (hd + 1) * LANES]
        qg_ref[0, :, sl] = _head_norm_rope(src, gqg, cg, sg, GQA_HEAD_DIM).astype(BF16)
    for g in range(GQA_KV_HEADS):
        sl = slice(g * LANES, (g + 1) * LANES)
        src = proj[:, _P_GK + g * LANES:_P_GK + (g + 1) * LANES]
        kg_ref[0, :, sl] = _head_norm_rope(src, gkg, cg, sg, GQA_HEAD_DIM).astype(BF16)
    vg_ref[0] = _with_denominator_column(proj[:, _P_GV:_P_GV + GQA_KV_HEADS * LANES]).astype(BF16)


def _const_spec(shape):
    nd = len(shape)
    return pl.BlockSpec(shape, lambda *_: (0,) * nd)


def _prep(x, g, win, qng, wuq, kvng, wukv, qhg, khg, gqg, gkg, cm, sm, cg, sg):
    b, s, d = x.shape
    t = TOKEN_TILE
    tok = lambda w: pl.BlockSpec((1, t, w), lambda bi, i: (bi, i, 0))
    tab = pl.BlockSpec((t, LANES), lambda bi, i: (i, 0))
    consts = (g, win, qng, wuq, kvng, wukv, qhg, khg, gqg, gkg)
    out_w = (MLA_HEADS * LANES, MLA_HEADS * LANES, MLA_HEADS * LANES,
             GQA_HEADS * LANES, GQA_KV_HEADS * LANES, GQA_KV_HEADS * LANES)
    return pl.pallas_call(
        _prep_body,
        grid=(b, s // t),
        in_specs=[tok(d)] + [_const_spec(c.shape) for c in consts] + [tab] * 4,
        out_specs=[tok(w) for w in out_w],
        out_shape=[jax.ShapeDtypeStruct((b, s, w), BF16) for w in out_w],
        compiler_params=pltpu.CompilerParams(
            dimension_semantics=("parallel", "parallel"), vmem_limit_bytes=VMEM_LIMIT),
        name="prep",
    )(x, *consts, cm, sm, cg, sg)


def _flash(q, k_ref, k_off, v_ref, v_off, seq_len, exp2_scale, track_max):
    rows = q.shape[0]
    nt_dims = (((1,), (1,)), ((), ()))

    def chunk(j):
        start = pl.multiple_of(j * KV_CHUNK, KV_CHUNK)
        kc = k_ref[0, pl.ds(start, KV_CHUNK), k_off:k_off + LANES]
        vc = v_ref[0, pl.ds(start, KV_CHUNK), v_off:v_off + LANES]
        return lax.dot_general(q, kc, nt_dims, preferred_element_type=F32), vc

    def step_plain(j, acc):
        s, vc = chunk(j)
        return acc + _dot(jnp.exp2(s * exp2_scale).astype(BF16), vc)

    def step_max(j, carry):
        m, acc = carry
        s, vc = chunk(j)
        m_new = jnp.maximum(m, jnp.max(s, axis=1, keepdims=True))
        alpha = jnp.exp2((m - m_new) * exp2_scale)
        p = jnp.exp2((s - m_new) * exp2_scale)
        return m_new, alpha * acc + _dot(p.astype(BF16), vc)

    acc0 = jnp.zeros((rows, LANES), F32)
    n_chunks = seq_len // KV_CHUNK
    if track_max:
        _, acc = lax.fori_loop(0, n_chunks, step_max, (jnp.full((rows, 1), -1e30, F32), acc0))
    else:
        acc = lax.fori_loop(0, n_chunks, step_plain, acc0)
    return acc / acc[:, DENOM_LANE:DENOM_LANE + 1]


def _pair_heads(o_even, o_odd):
    lane = lax.broadcasted_iota(jnp.int32, o_even.shape, 1)
    return jnp.where(lane < HEAD_V, o_even, pltpu.roll(o_odd, HEAD_V, 1))


def _attn_mla_body(q_ref, k_ref, v_ref, o_ref, *, seq_len, exp2_scale, track_max):
    o0 = _flash(q_ref[0, :, 0:LANES], k_ref, 0, v_ref, 0, seq_len, exp2_scale, track_max)
    o1 = _flash(q_ref[0, :, LANES:2 * LANES], k_ref, LANES, v_ref, LANES, seq_len, exp2_scale, track_max)
    o_ref[0] = _pair_heads(o0, o1).astype(o_ref.dtype)


def _attn_gqa_body(q_ref, k_ref, v_ref, o_ref, *, seq_len, exp2_scale, track_max):
    rep = GQA_HEADS // GQA_KV_HEADS
    tq = q_ref.shape[1]
    q = jnp.concatenate([q_ref[0, :, i * LANES:(i + 1) * LANES] for i in range(rep)], axis=0)
    o = _flash(q, k_ref, 0, v_ref, 0, seq_len, exp2_scale, track_max)
    for j in range(rep // 2):
        pair = _pair_heads(o[(2 * j) * tq:(2 * j + 1) * tq], o[(2 * j + 1) * tq:(2 * j + 2) * tq])
        o_ref[0, :, j * LANES:(j + 1) * LANES] = pair.astype(o_ref.dtype)


def _attn_mla(q, k, v, *, track_max):
    b, s, _ = q.shape
    tq = Q_TILE_MLA
    body = functools.partial(_attn_mla_body, seq_len=s, exp2_scale=MLA_QK ** -0.5 * math.log2(math.e),
                             track_max=track_max)
    return pl.pallas_call(
        body,
        grid=(b, MLA_HEADS // 2, s // tq),
        in_specs=[pl.BlockSpec((1, tq, 2 * LANES), lambda bi, hp, qi: (bi, qi, hp)),
                  pl.BlockSpec((1, s, 2 * LANES), lambda bi, hp, qi: (bi, 0, hp)),
                  pl.BlockSpec((1, s, 2 * LANES), lambda bi, hp, qi: (bi, 0, hp))],
        out_specs=pl.BlockSpec((1, tq, LANES), lambda bi, hp, qi: (bi, qi, hp)),
        out_shape=jax.ShapeDtypeStruct((b, s, MLA_HEADS * MLA_V), BF16),
        compiler_params=pltpu.CompilerParams(
            dimension_semantics=("parallel", "parallel", "arbitrary"), vmem_limit_bytes=VMEM_LIMIT),
        name="attn_mla_max" if track_max else "attn_mla",
    )(q, k, v)


def _attention(attn, q, k, v, head_dim, q_gain, k_gain):
    bound = math.sqrt(head_dim) * jnp.max(jnp.abs(q_gain)) * jnp.max(jnp.abs(k_gain))
    return lax.cond(bound <= MAX_SHIFT_FREE_SCORE,
                    functools.partial(attn, track_max=False), functools.partial(attn, track_max=True), q, k, v)


def _attn_gqa(q, k, v, *, track_max):
    b, s, _ = q.shape
    tq = Q_TILE_GQA
    rep = GQA_HEADS // GQA_KV_HEADS
    body = functools.partial(_attn_gqa_body, seq_len=s, exp2_scale=GQA_HEAD_DIM ** -0.5 * math.log2(math.e),
                             track_max=track_max)
    return pl.pallas_call(
        body,
        grid=(b, GQA_KV_HEADS, s // tq),
        in_specs=[pl.BlockSpec((1, tq, rep * LANES), lambda bi, g, qi: (bi, qi, g)),
                  pl.BlockSpec((1, s, LANES), lambda bi, g, qi: (bi, 0, g)),
                  pl.BlockSpec((1, s, LANES), lambda bi, g, qi: (bi, 0, g))],
        out_specs=pl.BlockSpec((1, tq, rep * GQA_HEAD_DIM), lambda bi, g, qi: (bi, qi, g)),
        out_shape=jax.ShapeDtypeStruct((b, s, GQA_HEADS * GQA_HEAD_DIM), BF16),
        compiler_params=pltpu.CompilerParams(
            dimension_semantics=("parallel", "parallel", "arbitrary"), vmem_limit_bytes=VMEM_LIMIT),
        name="attn_gqa_max" if track_max else "attn_gqa",
    )(q, k, v)


def _fill_normed_ext(h_scr, x, xp_ref, xn_ref, g):
    i = pl.program_id(1)
    t = x.shape[0]
    h_scr[HALO:HALO + t] = _rms(x, g).astype(BF16)
    hp = _rms(xp_ref[0], g)
    hn = _rms(xn_ref[0], g)
    h_scr[0:HALO] = jnp.where(i > 0, hp, 0.0).astype(BF16)
    h_scr[HALO + t:] = jnp.where(i < pl.num_programs(1) - 1, hn, 0.0).astype(BF16)


def _halo_specs(t, d):
    per = t // HALO
    main = pl.BlockSpec((1, t, d), lambda bi, i: (bi, i, 0))
    prev = pl.BlockSpec((1, HALO, d), lambda bi, i: (bi, jnp.maximum(i * per - 1, 0), 0))
    nxt = lambda n_blocks: pl.BlockSpec(
        (1, HALO, d), lambda bi, i: (bi, jnp.minimum((i + 1) * per, n_blocks - 1), 0))
    return main, prev, nxt


def _merge_body(x_ref, xp_ref, xn_ref, om_ref, og_ref, g_ref, wgate_ref, wpool_ref, poolw_ref, pscale_ref,
                wbm_ref, wbg_ref, wbp_ref, wout_ref, o_ref, h_scr, p_scr, *, seq_len):
    x = x_ref[0]
    t = x.shape[0]
    _fill_normed_ext(h_scr, x, xp_ref, xn_ref, g_ref[...])
    h = h_scr[HALO:HALO + t]
    p_scr[...] = _dot(h_scr[...], wpool_ref[...])
    pos = pl.program_id(1) * t + lax.broadcasted_iota(jnp.int32, (t, 1), 0)
    ys = []
    for gi, w in enumerate(POOL_WINDOWS):
        sl = slice(gi * POOL_GROUP, (gi + 1) * POOL_GROUP)
        tot = None
        for d in range(-(w // 2), w - w // 2):
            part = p_scr[HALO + d:HALO + d + t, sl]
            tot = part if tot is None else tot + part
        cnt = jnp.minimum(pos + (w - w // 2), seq_len) - jnp.maximum(pos - w // 2, 0)
        mixed = tot / cnt.astype(F32) - p_scr[HALO:HALO + t, sl]
        ys.append(_dot(mixed.astype(BF16), poolw_ref[gi]))
    y = (jnp.concatenate(ys, axis=1) * pscale_ref[...]).astype(BF16)
    gates = jax.nn.sigmoid(_dot(h, wgate_ref[...]))
    merged = (gates[:, 0:D_MODEL] * _dot(om_ref[0], wbm_ref[...])
              + gates[:, D_MODEL:2 * D_MODEL] * _dot(og_ref[0], wbg_ref[...])
              + gates[:, 2 * D_MODEL:3 * D_MODEL] * _dot(y, wbp_ref[...]))
    o_ref[0] = x + _dot(merged.astype(BF16), wout_ref[...])


def _merge(x, om, og, g, wgate, wpool, poolw, pscale, wbm, wbg, wbp, wout):
    b, s, d = x.shape
    t = TOKEN_TILE
    main, prev, nxt = _halo_specs(t, d)
    tok = lambda w: pl.BlockSpec((1, t, w), lambda bi, i: (bi, i, 0))
    consts = (g, wgate, wpool, poolw, pscale, wbm, wbg, wbp, wout)
    return pl.pallas_call(
        functools.partial(_merge_body, seq_len=s),
        grid=(b, s // t),
        in_specs=[main, prev, nxt(s // HALO), tok(om.shape[-1]), tok(og.shape[-1])]
                 + [_const_spec(c.shape) for c in consts],
        out_specs=main,
        out_shape=jax.ShapeDtypeStruct(x.shape, x.dtype),
        scratch_shapes=[pltpu.VMEM((t + 2 * HALO, d), BF16), pltpu.VMEM((t + 2 * HALO, POOL_WIDTH), F32)],
        compiler_params=pltpu.CompilerParams(
            dimension_semantics=("parallel", "parallel"), vmem_limit_bytes=VMEM_LIMIT),
        name="merge",
    )(x, x, x, om, og, *consts)


def _ffn_body(x_ref, xp_ref, xn_ref, g_ref, wup_ref, cw_ref, wdn_ref, o_ref, h_scr, u_scr, act_scr):
    x = x_ref[0]
    t = x.shape[0]
    _fill_normed_ext(h_scr, x, xp_ref, xn_ref, g_ref[...])

    def conv_half(k, u):
        u[...] = _dot(h_scr[...], wup_ref[k])
        cw = cw_ref[k]
        return (cw[0:1] * u[HALO - 1:HALO - 1 + t] + cw[1:2] * u[HALO:HALO + t]
                + cw[2:3] * u[HALO + 1:HALO + 1 + t] + cw[3:4])

    for c in range(N_FF_CHUNKS):
        u_gate = conv_half(c, u_scr.at[2 * (c % 2)])
        u_val = conv_half(c + N_FF_CHUNKS, u_scr.at[2 * (c % 2) + 1])
        act_scr[:, c * FF_CHUNK:(c + 1) * FF_CHUNK] = (jax.nn.silu(u_gate) * u_val).astype(BF16)
    o_ref[0] = x + _dot(act_scr[...], wdn_ref[...])


def _ffn(x, g, wup, cw, wdn):
    b, s, d = x.shape
    t = TOKEN_TILE
    main, prev, nxt = _halo_specs(t, d)
    consts = (g, wup, cw, wdn)
    return pl.pallas_call(
        _ffn_body,
        grid=(b, s // t),
        in_specs=[main, prev, nxt(s // HALO)] + [_const_spec(c.shape) for c in consts],
        out_specs=main,
        out_shape=jax.ShapeDtypeStruct(x.shape, x.dtype),
        scratch_shapes=[pltpu.VMEM((t + 2 * HALO, d), BF16), pltpu.VMEM((4, t + 2 * HALO, FF_CHUNK), F32),
                        pltpu.VMEM((t, D_FF), BF16)],
        compiler_params=pltpu.CompilerParams(
            dimension_semantics=("parallel", "parallel"), vmem_limit_bytes=VMEM_LIMIT),
        name="ffn",
    )(x, x, x, *consts)


def _pad_gain(g, perm):
    return _take_cols(g[None, :], perm)


def kernel(x, attn_norm_g, w_in, mla_q_norm_g, mla_w_uq, mla_kv_norm_g, mla_w_ukv, mla_q_head_g, mla_k_head_g, gqa_q_head_g, gqa_k_head_g, pool_w, pool_scale, w_branch_mla, w_branch_gqa, w_branch_pool, w_out, ffn_norm_g, ffn_w_up, ffn_conv_w, ffn_conv_b, ffn_w_down):
    b, s, d = x.shape
    assert d == D_MODEL and s % GRID_W == 0
    assert all(s % t == 0 for t in (TOKEN_TILE, Q_TILE_MLA, Q_TILE_GQA, KV_CHUNK)), s
    depth = w_in.shape[0]
    mla_perm, gqa_perm = _mla_head_perm(), _gqa_head_perm()
    cm, sm = _rope_tables(s, MLA_ROPE, mla_perm, MLA_NOPE)
    cg, sg = _rope_tables(s, GQA_HEAD_DIM, gqa_perm, 0)
    prep_cols, uq_cols, ukv_cols = _prep_in_cols(), _uq_cols(), _ukv_cols()
    row = lambda v: v[None, :].astype(F32)
    for l in range(depth):
        qm, km, vm, qg, kg, vg = _prep(
            x, row(attn_norm_g[l]), _take_cols(w_in[l], prep_cols).astype(BF16),
            row(mla_q_norm_g[l]), _take_cols(mla_w_uq[l], uq_cols).astype(BF16),
            row(mla_kv_norm_g[l]), _take_cols(mla_w_ukv[l], ukv_cols).astype(BF16),
            _pad_gain(mla_q_head_g[l], mla_perm), _pad_gain(mla_k_head_g[l], mla_perm),
            _pad_gain(gqa_q_head_g[l], gqa_perm), _pad_gain(gqa_k_head_g[l], gqa_perm),
            cm, sm, cg, sg)
        o_mla = _attention(_attn_mla, qm, km, vm, MLA_QK, mla_q_head_g[l], mla_k_head_g[l])
        o_gqa = _attention(_attn_gqa, qg, kg, vg, GQA_HEAD_DIM, gqa_q_head_g[l], gqa_k_head_g[l])
        x = _merge(
            x, o_mla, o_gqa, row(attn_norm_g[l]),
            w_in[l][:, _OFF_GATE:].astype(BF16), w_in[l][:, _OFF_POOL:_OFF_GATE].astype(BF16),
            pool_w[l].astype(BF16), row(pool_scale[l]),
            w_branch_mla[l].astype(BF16), w_branch_gqa[l].astype(BF16), w_branch_pool[l].astype(BF16),
            w_out[l].astype(BF16))
        wup = ffn_w_up[l].astype(BF16).reshape(d, 2 * N_FF_CHUNKS, FF_CHUNK).transpose(1, 0, 2)
        cw = jnp.concatenate([ffn_conv_w[l], ffn_conv_b[l][None, :], jnp.zeros((4, 2 * D_FF), F32)], axis=0)
        cw = cw.reshape(8, 2 * N_FF_CHUNKS, FF_CHUNK).transpose(1, 0, 2)
        wdn = ffn_w_down[l].astype(BF16)
        x = _ffn(x, row(ffn_norm_g[l]), wup, cw, wdn)
    return x
```

```python
import functools
import math

import numpy as np
import jax
import jax.numpy as jnp
from jax import lax
from jax.experimental import pallas as pl
from jax.experimental.pallas import tpu as pltpu

F32 = jnp.float32
BF16 = jnp.bfloat16

D_MODEL = 1024
GRID_W = 64
ROPE_THETA = 10000.0
EPS = 1e-6
MLA_HEADS = 8
MLA_Q_LORA = 256
MLA_KV_LORA = 128
MLA_NOPE = 64
MLA_ROPE = 32
MLA_V = 64
MLA_QK = MLA_NOPE + MLA_ROPE
GQA_HEADS = 8
GQA_KV_HEADS = 2
GQA_HEAD_DIM = 64
POOL_WINDOWS = (2, 4, 8, 16)
POOL_GROUPS = 4
POOL_GROUP = 128
POOL_WIDTH = POOL_GROUPS * POOL_GROUP
N_BRANCH = 3
D_FF = 2816

LANES = 128
ROPE_LANE_SHIFT = 64
HALO = 16
HEAD_V = 64
DENOM_LANE = HEAD_V
VMEM_LIMIT = 56 * 1024 * 1024

TOKEN_TILE = 512
Q_TILE_MLA = 1024
Q_TILE_GQA = 256
KV_CHUNK = 1024
FF_CHUNK = 256
N_FF_CHUNKS = D_FF // FF_CHUNK

MAX_SHIFT_FREE_SCORE = 60.0

_OFF_CQ = 0
_OFF_CKV = _OFF_CQ + MLA_Q_LORA
_OFF_KROPE = _OFF_CKV + MLA_KV_LORA
_OFF_GQ = _OFF_KROPE + MLA_ROPE
_OFF_GK = _OFF_GQ + GQA_HEADS * GQA_HEAD_DIM
_OFF_GV = _OFF_GK + GQA_KV_HEADS * GQA_HEAD_DIM
_OFF_POOL = _OFF_GV + GQA_KV_HEADS * GQA_HEAD_DIM
_OFF_GATE = _OFF_POOL + POOL_WIDTH


def _mla_head_perm():
    perm = np.full((LANES,), -1, np.int64)
    half = MLA_ROPE // 2
    perm[0:half] = MLA_NOPE + np.arange(half)
    perm[ROPE_LANE_SHIFT:ROPE_LANE_SHIFT + half] = MLA_NOPE + half + np.arange(half)
    perm[half:ROPE_LANE_SHIFT] = np.arange(ROPE_LANE_SHIFT - half)
    rest = MLA_NOPE - (ROPE_LANE_SHIFT - half)
    perm[ROPE_LANE_SHIFT + half:ROPE_LANE_SHIFT + half + rest] = (ROPE_LANE_SHIFT - half) + np.arange(rest)
    return perm


def _gqa_head_perm():
    perm = np.full((LANES,), -1, np.int64)
    half = GQA_HEAD_DIM // 2
    perm[0:half] = np.arange(half)
    perm[ROPE_LANE_SHIFT:ROPE_LANE_SHIFT + half] = half + np.arange(half)
    return perm


def _take_cols(w, idx):
    idx = np.asarray(idx)
    cols = jnp.take(w, jnp.asarray(np.maximum(idx, 0)), axis=-1)
    return jnp.where(jnp.asarray(idx >= 0), cols, jnp.zeros((), w.dtype))


def _prep_in_cols():
    mla = _mla_head_perm()
    gqa = _gqa_head_perm()
    cols = [np.arange(_OFF_CQ, _OFF_CQ + MLA_Q_LORA), np.arange(_OFF_CKV, _OFF_CKV + MLA_KV_LORA)]
    kr = np.where(mla >= MLA_NOPE, _OFF_KROPE + (mla - MLA_NOPE), -1)
    cols.append(kr)
    for h in range(GQA_HEADS):
        cols.append(np.where(gqa >= 0, _OFF_GQ + h * GQA_HEAD_DIM + gqa, -1))
    for g in range(GQA_KV_HEADS):
        cols.append(np.where(gqa >= 0, _OFF_GK + g * GQA_HEAD_DIM + gqa, -1))
    for g in range(GQA_KV_HEADS):
        v = np.full((LANES,), -1, np.int64)
        v[:GQA_HEAD_DIM] = _OFF_GV + g * GQA_HEAD_DIM + np.arange(GQA_HEAD_DIM)
        cols.append(v)
    return np.concatenate(cols)


_PREP_W = MLA_Q_LORA + MLA_KV_LORA + LANES + (GQA_HEADS + 2 * GQA_KV_HEADS) * LANES
_P_CQ = 0
_P_CKV = MLA_Q_LORA
_P_KROPE = _P_CKV + MLA_KV_LORA
_P_GQ = _P_KROPE + LANES
_P_GK = _P_GQ + GQA_HEADS * LANES
_P_GV = _P_GK + GQA_KV_HEADS * LANES


def _uq_cols():
    mla = _mla_head_perm()
    return np.concatenate([np.where(mla >= 0, h * MLA_QK + mla, -1) for h in range(MLA_HEADS)])


def _ukv_cols():
    mla = _mla_head_perm()
    per_head = MLA_NOPE + MLA_V
    k = [np.where((mla >= 0) & (mla < MLA_NOPE), h * per_head + mla, -1) for h in range(MLA_HEADS)]
    v = []
    for h in range(MLA_HEADS):
        vh = np.full((LANES,), -1, np.int64)
        vh[:MLA_V] = h * per_head + MLA_NOPE + np.arange(MLA_V)
        v.append(vh)
    return np.concatenate(k + v)


def _rope_tables(seq_len, rot_dim, perm, first_rope_dim):
    rows = seq_len // GRID_W
    row_idx = jnp.repeat(jnp.arange(rows, dtype=F32), GRID_W)
    col_idx = jnp.tile(jnp.arange(GRID_W, dtype=F32), rows)
    n_axis = rot_dim // 4
    inv_freq = ROPE_THETA ** (-jnp.arange(n_axis, dtype=F32) / n_axis)
    ang = jnp.concatenate([row_idx[:, None] * inv_freq, col_idx[:, None] * inv_freq], axis=-1)
    cos, sin = jnp.cos(ang), jnp.sin(ang)
    half = rot_dim // 2
    rel = perm - first_rope_dim
    is_rope = (perm >= first_rope_dim)
    freq = np.where(is_rope, rel % half, 0)
    sign = np.where(is_rope, np.where(rel < half, -1.0, 1.0), 0.0).astype(np.float32)
    c = jnp.where(jnp.asarray(is_rope), jnp.take(cos, jnp.asarray(freq), axis=1), 1.0)
    s = jnp.take(sin, jnp.asarray(freq), axis=1) * jnp.asarray(sign)
    return c.astype(F32), s.astype(F32)


def _rms(x, g):
    return x * lax.rsqrt(jnp.mean(x * x, axis=-1, keepdims=True) + EPS) * g


def _dot(a, b):
    return jnp.dot(a, b, preferred_element_type=F32)


def _per_head_ones():
    head = lambda axis: lax.broadcasted_iota(jnp.int32, (2 * LANES, 2 * LANES), axis) // LANES
    return jnp.where(head(0) == head(1), 1.0, 0.0).astype(BF16)


def _head_pair_norm_rope(xp, g, cos, sin, dim, ones_bd):
    sq = xp * xp
    hi = sq.astype(BF16)
    lo = (sq - hi.astype(F32)).astype(BF16)
    ms = (_dot(hi, ones_bd) + _dot(lo, ones_bd)) * (1.0 / dim)
    outs = []
    for i in range(2):
        sl = slice(i * LANES, (i + 1) * LANES)
        xn = xp[:, sl] * lax.rsqrt(ms[:, sl] + EPS) * g
        outs.append((xn * cos + pltpu.roll(xn, ROPE_LANE_SHIFT, 1) * sin).astype(BF16))
    return jnp.concatenate(outs, axis=1)


def _with_denominator_column(v):
    lane = lax.broadcasted_iota(jnp.int32, v.shape, 1)
    return jnp.where((lane & (LANES - 1)) == DENOM_LANE, 1.0, v)


def _prep_body(x_ref, g_ref, win_ref, qng_ref, wuq_ref, kvng_ref, wukv_ref, qhg_ref, khg_ref,
               gqg_ref, gkg_ref, cm_ref, sm_ref, cg_ref, sg_ref,
               qm_ref, km_ref, vm_ref, qg_ref, kg_ref, vg_ref):
    h = _rms(x_ref[0], g_ref[...]).astype(BF16)
    proj = _dot(h, win_ref[...])
    cq = _rms(proj[:, _P_CQ:_P_CQ + MLA_Q_LORA], qng_ref[...]).astype(BF16)
    ckv = _rms(proj[:, _P_CKV:_P_CKV + MLA_KV_LORA], kvng_ref[...]).astype(BF16)
    krope = proj[:, _P_KROPE:_P_KROPE + LANES]
    q = _dot(cq, wuq_ref[...])
    kv = _dot(ckv, wukv_ref[...])
    cm, sm, cg, sg = cm_ref[...], sm_ref[...], cg_ref[...], sg_ref[...]
    qhg, khg, gqg, gkg = qhg_ref[...], khg_ref[...], gqg_ref[...], gkg_ref[...]
    ones_bd = _per_head_ones()
    krope2 = jnp.concatenate([krope, krope], axis=1)
    for pr in range(MLA_HEADS // 2):
        sl = slice(2 * pr * LANES, 2 * (pr + 1) * LANES)
        qm_ref[0, :, sl] = _head_pair_norm_rope(q[:, sl], qhg, cm, sm, MLA_QK, ones_bd)
        km_ref[0, :, sl] = _head_pair_norm_rope(kv[:, sl] + krope2, khg, cm, sm, MLA_QK, ones_bd)
    vm_ref[0] = _with_denominator_column(kv[:, MLA_HEADS * LANES:]).astype(BF16)
    for pr in range(GQA_HEADS // 2):
        sl = slice(2 * pr * LANES, 2 * (pr + 1) * LANES)
        src = proj[:, _P_GQ + sl.start:_P_GQ + sl.stop]
        qg_ref[0, :, sl] = _head_pair_norm_rope(src, gqg, cg, sg, GQA_HEAD_DIM, ones_bd)
    for pr in range(GQA_KV_HEADS // 2):
        sl = slice(2 * pr * LANES, 2 * (pr + 1) * LANES)
        src = proj[:, _P_GK + sl.start:_P_GK + sl.stop]
        kg_ref[0, :, sl] = _head_pair_norm_rope(src, gkg, cg, sg, GQA_HEAD_DIM, ones_bd)
    vg_ref[0] = _with_denominator_column(proj[:, _P_GV:_P_GV + GQA_KV_HEADS * LANES]).astype(BF16)


def _const_spec(shape):
    nd = len(shape)
    return pl.BlockSpec(shape, lambda *_: (0,) * nd)


def _prep(x, g, win, qng, wuq, kvng, wukv, qhg, khg, gqg, gkg, cm, sm, cg, sg):
    b, s, d = x.shape
    t = TOKEN_TILE
    tok = lambda w: pl.BlockSpec((1, t, w), lambda bi, i: (bi, i, 0))
    tab = pl.BlockSpec((t, LANES), lambda bi, i: (i, 0))
    consts = (g, win, qng, wuq, kvng, wukv, qhg, khg, gqg, gkg)
    out_w = (MLA_HEADS * LANES, MLA_HEADS * LANES, MLA_HEADS * LANES,
             GQA_HEADS * LANES, GQA_KV_HEADS * LANES, GQA_KV_HEADS * LANES)
    return pl.pallas_call(
        _prep_body,
        grid=(b, s // t),
        in_specs=[tok(d)] + [_const_spec(c.shape) for c in consts] + [tab] * 4,
        out_specs=[tok(w) for w in out_w],
        out_shape=[jax.ShapeDtypeStruct((b, s, w), BF16) for w in out_w],
        compiler_params=pltpu.CompilerParams(
            dimension_semantics=("parallel", "parallel"), vmem_limit_bytes=VMEM_LIMIT),
        name="prep",
    )(x, *consts, cm, sm, cg, sg)


def _flash(q, k_ref, k_off, vt_ref, seq_len, exp2_scale, track_max):
    rows = q.shape[0]
    nt_dims = (((1,), (1,)), ((), ()))

    def chunk(j):
        start = pl.multiple_of(j * KV_CHUNK, KV_CHUNK)
        kc = k_ref[0, pl.ds(start, KV_CHUNK), k_off:k_off + LANES]
        return lax.dot_general(kc, q, nt_dims, preferred_element_type=F32), vt_ref[j]

    def step_plain(j, acc):
        st, vt = chunk(j)
        return acc + _dot(vt, jnp.exp2(st * exp2_scale).astype(BF16))

    def step_max(j, carry):
        m, acc = carry
        st, vt = chunk(j)
        m_new = jnp.maximum(m, jnp.max(st, axis=0, keepdims=True))
        alpha = jnp.exp2((m - m_new) * exp2_scale)
        p = jnp.exp2((st - m_new) * exp2_scale)
        return m_new, alpha * acc + _dot(vt, p.astype(BF16))

    acc0 = jnp.zeros((LANES, rows), F32)
    n_chunks = seq_len // KV_CHUNK
    if track_max:
        _, acc = lax.fori_loop(0, n_chunks, step_max, (jnp.full((1, rows), -1e30, F32), acc0))
    else:
        acc = lax.fori_loop(0, n_chunks, step_plain, acc0, unroll=True)
    return (acc / acc[DENOM_LANE:DENOM_LANE + 1, :]).T


def _pair_heads(o_even, o_odd):
    lane = lax.broadcasted_iota(jnp.int32, o_even.shape, 1)
    return jnp.where(lane < HEAD_V, o_even, pltpu.roll(o_odd, HEAD_V, 1))


def _attn_mla_body(q_ref, k_ref, vt_ref, o_ref, *, seq_len, exp2_scale, track_max):
    o0 = _flash(q_ref[0, :, 0:LANES], k_ref, 0, vt_ref.at[0, 0], seq_len, exp2_scale, track_max)
    o1 = _flash(q_ref[0, :, LANES:2 * LANES], k_ref, LANES, vt_ref.at[0, 1], seq_len, exp2_scale, track_max)
    o_ref[0] = _pair_heads(o0, o1).astype(o_ref.dtype)


def _attn_gqa_body(q_ref, k_ref, vt_ref, o_ref, *, seq_len, exp2_scale, track_max):
    rep = GQA_HEADS // GQA_KV_HEADS
    tq = q_ref.shape[1]
    q = jnp.concatenate([q_ref[0, :, i * LANES:(i + 1) * LANES] for i in range(rep)], axis=0)
    o = _flash(q, k_ref, 0, vt_ref.at[0, 0], seq_len, exp2_scale, track_max)
    for j in range(rep // 2):
        pair = _pair_heads(o[(2 * j) * tq:(2 * j + 1) * tq], o[(2 * j + 1) * tq:(2 * j + 2) * tq])
        o_ref[0, :, j * LANES:(j + 1) * LANES] = pair.astype(o_ref.dtype)


def _values_transposed(v, heads):
    b, s, _ = v.shape
    return v.reshape(b, s // KV_CHUNK, KV_CHUNK, heads, LANES).transpose(0, 3, 1, 4, 2)


def _vt_spec(heads_per_step, n_chunks):
    return pl.BlockSpec((1, heads_per_step, n_chunks, LANES, KV_CHUNK), lambda bi, h, qi: (bi, h, 0, 0, 0))


def _attn_mla(q, k, v, *, track_max):
    b, s, _ = q.shape
    tq = Q_TILE_MLA
    body = functools.partial(_attn_mla_body, seq_len=s, exp2_scale=MLA_QK ** -0.5 * math.log2(math.e),
                             track_max=track_max)
    return pl.pallas_call(
        body,
        grid=(b, MLA_HEADS // 2, s // tq),
        in_specs=[pl.BlockSpec((1, tq, 2 * LANES), lambda bi, hp, qi: (bi, qi, hp)),
                  pl.BlockSpec((1, s, 2 * LANES), lambda bi, hp, qi: (bi, 0, hp)),
                  _vt_spec(2, s // KV_CHUNK)],
        out_specs=pl.BlockSpec((1, tq, LANES), lambda bi, hp, qi: (bi, qi, hp)),
        out_shape=jax.ShapeDtypeStruct((b, s, MLA_HEADS * MLA_V), BF16),
        compiler_params=pltpu.CompilerParams(
            dimension_semantics=("parallel", "parallel", "arbitrary"), vmem_limit_bytes=VMEM_LIMIT),
        name="attn_mla_max" if track_max else "attn_mla",
    )(q, k, v)


def _attention(attn, q, k, v, head_dim, q_gain, k_gain):
    bound = math.sqrt(head_dim) * jnp.max(jnp.abs(q_gain)) * jnp.max(jnp.abs(k_gain))
    return lax.cond(bound <= MAX_SHIFT_FREE_SCORE,
                    functools.partial(attn, track_max=False), functools.partial(attn, track_max=True), q, k, v)


def _attn_gqa(q, k, v, *, track_max):
    b, s, _ = q.shape
    tq = Q_TILE_GQA
    rep = GQA_HEADS // GQA_KV_HEADS
    body = functools.partial(_attn_gqa_body, seq_len=s, exp2_scale=GQA_HEAD_DIM ** -0.5 * math.log2(math.e),
                             track_max=track_max)
    return pl.pallas_call(
        body,
        grid=(b, GQA_KV_HEADS, s // tq),
        in_specs=[pl.BlockSpec((1, tq, rep * LANES), lambda bi, g, qi: (bi, qi, g)),
                  pl.BlockSpec((1, s, LANES), lambda bi, g, qi: (bi, 0, g)),
                  _vt_spec(1, s // KV_CHUNK)],
        out_specs=pl.BlockSpec((1, tq, rep * GQA_HEAD_DIM), lambda bi, g, qi: (bi, qi, g)),
        out_shape=jax.ShapeDtypeStruct((b, s, GQA_HEADS * GQA_HEAD_DIM), BF16),
        compiler_params=pltpu.CompilerParams(
            dimension_semantics=("parallel", "parallel", "arbitrary"), vmem_limit_bytes=VMEM_LIMIT),
        name="attn_gqa_max" if track_max else "attn_gqa",
    )(q, k, v)


def _fill_normed_ext(h_scr, x, xp_ref, xn_ref, g):
    i = pl.program_id(1)
    t = x.shape[0]
    h_scr[HALO:HALO + t] = _rms(x, g).astype(BF16)
    hp = _rms(xp_ref[0], g)
    hn = _rms(xn_ref[0], g)
    h_scr[0:HALO] = jnp.where(i > 0, hp, 0.0).astype(BF16)
    h_scr[HALO + t:] = jnp.where(i < pl.num_programs(1) - 1, hn, 0.0).astype(BF16)


def _halo_specs(t, d):
    per = t // HALO
    main = pl.BlockSpec((1, t, d), lambda bi, i: (bi, i, 0))
    prev = pl.BlockSpec((1, HALO, d), lambda bi, i: (bi, jnp.maximum(i * per - 1, 0), 0))
    nxt = lambda n_blocks: pl.BlockSpec(
        (1, HALO, d), lambda bi, i: (bi, jnp.minimum((i + 1) * per, n_blocks - 1), 0))
    return main, prev, nxt


def _merge_body(x_ref, xp_ref, xn_ref, om_ref, og_ref, g_ref, wgate_ref, wpool_ref, poolw_ref, pscale_ref,
                wbm_ref, wbg_ref, wbp_ref, wout_ref, o_ref, h_scr, p_scr, *, seq_len):
    x = x_ref[0]
    t = x.shape[0]
    _fill_normed_ext(h_scr, x, xp_ref, xn_ref, g_ref[...])
    h = h_scr[HALO:HALO + t]
    p_scr[...] = _dot(h_scr[...], wpool_ref[...])
    pos = pl.program_id(1) * t + lax.broadcasted_iota(jnp.int32, (t, 1), 0)
    ys = []
    for gi, w in enumerate(POOL_WINDOWS):
        sl = slice(gi * POOL_GROUP, (gi + 1) * POOL_GROUP)
        tot = None
        for d in range(-(w // 2), w - w // 2):
            part = p_scr[HALO + d:HALO + d + t, sl]
            tot = part if tot is None else tot + part
        cnt = jnp.minimum(pos + (w - w // 2), seq_len) - jnp.maximum(pos - w // 2, 0)
        mixed = tot / cnt.astype(F32) - p_scr[HALO:HALO + t, sl]
        ys.append(_dot(mixed.astype(BF16), poolw_ref[gi]))
    y = (jnp.concatenate(ys, axis=1) * pscale_ref[...]).astype(BF16)
    gates = jax.nn.sigmoid(_dot(h, wgate_ref[...]))
    merged = (gates[:, 0:D_MODEL] * _dot(om_ref[0], wbm_ref[...])
              + gates[:, D_MODEL:2 * D_MODEL] * _dot(og_ref[0], wbg_ref[...])
              + gates[:, 2 * D_MODEL:3 * D_MODEL] * _dot(y, wbp_ref[...]))
    o_ref[0] = x + _dot(merged.astype(BF16), wout_ref[...])


def _merge(x, om, og, g, wgate, wpool, poolw, pscale, wbm, wbg, wbp, wout):
    b, s, d = x.shape
    t = TOKEN_TILE
    main, prev, nxt = _halo_specs(t, d)
    tok = lambda w: pl.BlockSpec((1, t, w), lambda bi, i: (bi, i, 0))
    consts = (g, wgate, wpool, poolw, pscale, wbm, wbg, wbp, wout)
    return pl.pallas_call(
        functools.partial(_merge_body, seq_len=s),
        grid=(b, s // t),
        in_specs=[main, prev, nxt(s // HALO), tok(om.shape[-1]), tok(og.shape[-1])]
                 + [_const_spec(c.shape) for c in consts],
        out_specs=main,
        out_shape=jax.ShapeDtypeStruct(x.shape, x.dtype),
        scratch_shapes=[pltpu.VMEM((t + 2 * HALO, d), BF16), pltpu.VMEM((t + 2 * HALO, POOL_WIDTH), F32)],
        compiler_params=pltpu.CompilerParams(
            dimension_semantics=("parallel", "parallel"), vmem_limit_bytes=VMEM_LIMIT),
        name="merge",
    )(x, x, x, om, og, *consts)


def _ffn_body(x_ref, xp_ref, xn_ref, g_ref, wup_ref, cw_ref, wdn_ref, o_ref, h_scr, u_scr, act_scr):
    x = x_ref[0]
    t = x.shape[0]
    _fill_normed_ext(h_scr, x, xp_ref, xn_ref, g_ref[...])

    def conv_half(k, u):
        u[...] = _dot(h_scr[...], wup_ref[k])
        cw = cw_ref[k]
        return (cw[0:1] * u[HALO - 1:HALO - 1 + t] + cw[1:2] * u[HALO:HALO + t]
                + cw[2:3] * u[HALO + 1:HALO + 1 + t] + cw[3:4])

    for c in range(N_FF_CHUNKS):
        u_gate = conv_half(c, u_scr.at[2 * (c % 2)])
        u_val = conv_half(c + N_FF_CHUNKS, u_scr.at[2 * (c % 2) + 1])
        act_scr[:, c * FF_CHUNK:(c + 1) * FF_CHUNK] = (jax.nn.silu(u_gate) * u_val).astype(BF16)
    o_ref[0] = x + _dot(act_scr[...], wdn_ref[...])


def _ffn(x, g, wup, cw, wdn):
    b, s, d = x.shape
    t = TOKEN_TILE
    main, prev, nxt = _halo_specs(t, d)
    consts = (g, wup, cw, wdn)
    return pl.pallas_call(
        _ffn_body,
        grid=(b, s // t),
        in_specs=[main, prev, nxt(s // HALO)] + [_const_spec(c.shape) for c in consts],
        out_specs=main,
        out_shape=jax.ShapeDtypeStruct(x.shape, x.dtype),
        scratch_shapes=[pltpu.VMEM((t + 2 * HALO, d), BF16), pltpu.VMEM((4, t + 2 * HALO, FF_CHUNK), F32),
                        pltpu.VMEM((t, D_FF), BF16)],
        compiler_params=pltpu.CompilerParams(
            dimension_semantics=("parallel", "parallel"), vmem_limit_bytes=VMEM_LIMIT),
        name="ffn",
    )(x, x, x, *consts)


def _pad_gain(g, perm):
    return _take_cols(g[None, :], perm)


def kernel(x, attn_norm_g, w_in, mla_q_norm_g, mla_w_uq, mla_kv_norm_g, mla_w_ukv, mla_q_head_g, mla_k_head_g, gqa_q_head_g, gqa_k_head_g, pool_w, pool_scale, w_branch_mla, w_branch_gqa, w_branch_pool, w_out, ffn_norm_g, ffn_w_up, ffn_conv_w, ffn_conv_b, ffn_w_down):
    b, s, d = x.shape
    assert d == D_MODEL and s % GRID_W == 0
    assert all(s % t == 0 for t in (TOKEN_TILE, Q_TILE_MLA, Q_TILE_GQA, KV_CHUNK)), s
    depth = w_in.shape[0]
    mla_perm, gqa_perm = _mla_head_perm(), _gqa_head_perm()
    cm, sm = _rope_tables(s, MLA_ROPE, mla_perm, MLA_NOPE)
    cg, sg = _rope_tables(s, GQA_HEAD_DIM, gqa_perm, 0)
    prep_cols, uq_cols, ukv_cols = _prep_in_cols(), _uq_cols(), _ukv_cols()
    row = lambda v: v[None, :].astype(F32)
    for l in range(depth):
        qm, km, vm, qg, kg, vg = _prep(
            x, row(attn_norm_g[l]), _take_cols(w_in[l], prep_cols).astype(BF16),
            row(mla_q_norm_g[l]), _take_cols(mla_w_uq[l], uq_cols).astype(BF16),
            row(mla_kv_norm_g[l]), _take_cols(mla_w_ukv[l], ukv_cols).astype(BF16),
            _pad_gain(mla_q_head_g[l], mla_perm), _pad_gain(mla_k_head_g[l], mla_perm),
            _pad_gain(gqa_q_head_g[l], gqa_perm), _pad_gain(gqa_k_head_g[l], gqa_perm),
            cm, sm, cg, sg)
        o_mla = _attention(_attn_mla, qm, km, _values_transposed(vm, MLA_HEADS), MLA_QK,
                           mla_q_head_g[l], mla_k_head_g[l])
        o_gqa = _attention(_attn_gqa, qg, kg, _values_transposed(vg, GQA_KV_HEADS), GQA_HEAD_DIM,
                           gqa_q_head_g[l], gqa_k_head_g[l])
        x = _merge(
            x, o_mla, o_gqa, row(attn_norm_g[l]),
            w_in[l][:, _OFF_GATE:].astype(BF16), w_in[l][:, _OFF_POOL:_OFF_GATE].astype(BF16),
            pool_w[l].astype(BF16), row(pool_scale[l]),
            w_branch_mla[l].astype(BF16), w_branch_gqa[l].astype(BF16), w_branch_pool[l].astype(BF16),
            w_out[l].astype(BF16))
        wup = ffn_w_up[l].astype(BF16).reshape(d, 2 * N_FF_CHUNKS, FF_CHUNK).transpose(1, 0, 2)
        cw = jnp.concatenate([ffn_conv_w[l], ffn_conv_b[l][None, :], jnp.zeros((4, 2 * D_FF), F32)], axis=0)
        cw = cw.reshape(8, 2 * N_FF_CHUNKS, FF_CHUNK).transpose(1, 0, 2)
        wdn = ffn_w_down[l].astype(BF16)
        x = _ffn(x, row(ffn_norm_g[l]), wup, cw, wdn)
    return x
```

```python
import functools
import math

import numpy as np
import jax
import jax.numpy as jnp
from jax import lax
from jax.experimental import pallas as pl
from jax.experimental.pallas import tpu as pltpu

F32 = jnp.float32
BF16 = jnp.bfloat16

D_MODEL = 1024
GRID_W = 64
ROPE_THETA = 10000.0
EPS = 1e-6
MLA_HEADS = 8
MLA_Q_LORA = 256
MLA_KV_LORA = 128
MLA_NOPE = 64
MLA_ROPE = 32
MLA_V = 64
MLA_QK = MLA_NOPE + MLA_ROPE
GQA_HEADS = 8
GQA_KV_HEADS = 2
GQA_HEAD_DIM = 64
POOL_WINDOWS = (2, 4, 8, 16)
POOL_GROUPS = 4
POOL_GROUP = 128
POOL_WIDTH = POOL_GROUPS * POOL_GROUP
N_BRANCH = 3
D_FF = 2816

LANES = 128
ROPE_LANE_SHIFT = 64
HALO = 16
HEAD_V = 64
VMEM_LIMIT = 56 * 1024 * 1024

TOKEN_TILE = 512
Q_TILE_MLA = 2048
Q_TILE_GQA = 512
KV_CHUNK = 1024
FF_CHUNK = 256
N_FF_CHUNKS = D_FF // FF_CHUNK

MAX_SHIFT_FREE_SCORE = 60.0

_OFF_CQ = 0
_OFF_CKV = _OFF_CQ + MLA_Q_LORA
_OFF_KROPE = _OFF_CKV + MLA_KV_LORA
_OFF_GQ = _OFF_KROPE + MLA_ROPE
_OFF_GK = _OFF_GQ + GQA_HEADS * GQA_HEAD_DIM
_OFF_GV = _OFF_GK + GQA_KV_HEADS * GQA_HEAD_DIM
_OFF_POOL = _OFF_GV + GQA_KV_HEADS * GQA_HEAD_DIM
_OFF_GATE = _OFF_POOL + POOL_WIDTH


def _mla_head_perm():
    perm = np.full((LANES,), -1, np.int64)
    half = MLA_ROPE // 2
    perm[0:half] = MLA_NOPE + np.arange(half)
    perm[ROPE_LANE_SHIFT:ROPE_LANE_SHIFT + half] = MLA_NOPE + half + np.arange(half)
    perm[half:ROPE_LANE_SHIFT] = np.arange(ROPE_LANE_SHIFT - half)
    rest = MLA_NOPE - (ROPE_LANE_SHIFT - half)
    perm[ROPE_LANE_SHIFT + half:ROPE_LANE_SHIFT + half + rest] = (ROPE_LANE_SHIFT - half) + np.arange(rest)
    return perm


def _gqa_head_perm():
    perm = np.full((LANES,), -1, np.int64)
    half = GQA_HEAD_DIM // 2
    perm[0:half] = np.arange(half)
    perm[ROPE_LANE_SHIFT:ROPE_LANE_SHIFT + half] = half + np.arange(half)
    return perm


def _take_cols(w, idx):
    idx = np.asarray(idx)
    cols = jnp.take(w, jnp.asarray(np.maximum(idx, 0)), axis=-1)
    return jnp.where(jnp.asarray(idx >= 0), cols, jnp.zeros((), w.dtype))


def _prep_in_cols():
    mla = _mla_head_perm()
    gqa = _gqa_head_perm()
    cols = [np.arange(_OFF_CQ, _OFF_CQ + MLA_Q_LORA), np.arange(_OFF_CKV, _OFF_CKV + MLA_KV_LORA)]
    kr = np.where(mla >= MLA_NOPE, _OFF_KROPE + (mla - MLA_NOPE), -1)
    cols.append(kr)
    for h in range(GQA_HEADS):
        cols.append(np.where(gqa >= 0, _OFF_GQ + h * GQA_HEAD_DIM + gqa, -1))
    for g in range(GQA_KV_HEADS):
        cols.append(np.where(gqa >= 0, _OFF_GK + g * GQA_HEAD_DIM + gqa, -1))
    cols.append(np.arange(_OFF_GV, _OFF_GV + GQA_KV_HEADS * GQA_HEAD_DIM))
    return np.concatenate(cols)


_PREP_W = (MLA_Q_LORA + MLA_KV_LORA + LANES + (GQA_HEADS + GQA_KV_HEADS) * LANES
           + GQA_KV_HEADS * GQA_HEAD_DIM)
_P_CQ = 0
_P_CKV = MLA_Q_LORA
_P_KROPE = _P_CKV + MLA_KV_LORA
_P_GQ = _P_KROPE + LANES
_P_GK = _P_GQ + GQA_HEADS * LANES
_P_GV = _P_GK + GQA_KV_HEADS * LANES


def _uq_cols():
    mla = _mla_head_perm()
    return np.concatenate([np.where(mla >= 0, h * MLA_QK + mla, -1) for h in range(MLA_HEADS)])


def _ukv_cols():
    mla = _mla_head_perm()
    per_head = MLA_NOPE + MLA_V
    k = [np.where((mla >= 0) & (mla < MLA_NOPE), h * per_head + mla, -1) for h in range(MLA_HEADS)]
    v = [h * per_head + MLA_NOPE + np.arange(MLA_V) for h in range(MLA_HEADS)]
    return np.concatenate(k + v)


def _rope_tables(seq_len, rot_dim, perm, first_rope_dim):
    rows = seq_len // GRID_W
    row_idx = jnp.repeat(jnp.arange(rows, dtype=F32), GRID_W)
    col_idx = jnp.tile(jnp.arange(GRID_W, dtype=F32), rows)
    n_axis = rot_dim // 4
    inv_freq = ROPE_THETA ** (-jnp.arange(n_axis, dtype=F32) / n_axis)
    ang = jnp.concatenate([row_idx[:, None] * inv_freq, col_idx[:, None] * inv_freq], axis=-1)
    cos, sin = jnp.cos(ang), jnp.sin(ang)
    half = rot_dim // 2
    rel = perm - first_rope_dim
    is_rope = (perm >= first_rope_dim)
    freq = np.where(is_rope, rel % half, 0)
    sign = np.where(is_rope, np.where(rel < half, -1.0, 1.0), 0.0).astype(np.float32)
    c = jnp.where(jnp.asarray(is_rope), jnp.take(cos, jnp.asarray(freq), axis=1), 1.0)
    s = jnp.take(sin, jnp.asarray(freq), axis=1) * jnp.asarray(sign)
    return c.astype(F32), s.astype(F32)


def _rms(x, g):
    return x * lax.rsqrt(jnp.mean(x * x, axis=-1, keepdims=True) + EPS) * g


def _dot(a, b):
    return jnp.dot(a, b, preferred_element_type=F32)


def _per_head_ones():
    head = lambda axis: lax.broadcasted_iota(jnp.int32, (2 * LANES, 2 * LANES), axis) // LANES
    return jnp.where(head(0) == head(1), 1.0, 0.0).astype(BF16)


def _head_pair_norm_rope(xp, g, cos, sin, dim, ones_bd):
    sq = xp * xp
    hi = sq.astype(BF16)
    lo = (sq - hi.astype(F32)).astype(BF16)
    ms = (_dot(hi, ones_bd) + _dot(lo, ones_bd)) * (1.0 / dim)
    outs = []
    for i in range(2):
        sl = slice(i * LANES, (i + 1) * LANES)
        xn = xp[:, sl] * lax.rsqrt(ms[:, sl] + EPS) * g
        outs.append((xn * cos + pltpu.roll(xn, ROPE_LANE_SHIFT, 1) * sin).astype(BF16))
    return jnp.concatenate(outs, axis=1)


def _prep_body(x_ref, g_ref, win_ref, qng_ref, wuq_ref, kvng_ref, wukv_ref, qhg_ref, khg_ref,
               gqg_ref, gkg_ref, cm_ref, sm_ref, cg_ref, sg_ref,
               qm_ref, km_ref, vm_ref, qg_ref, kg_ref, vg_ref):
    h = _rms(x_ref[0], g_ref[...]).astype(BF16)
    proj = _dot(h, win_ref[...])
    cq = _rms(proj[:, _P_CQ:_P_CQ + MLA_Q_LORA], qng_ref[...]).astype(BF16)
    ckv = _rms(proj[:, _P_CKV:_P_CKV + MLA_KV_LORA], kvng_ref[...]).astype(BF16)
    krope = proj[:, _P_KROPE:_P_KROPE + LANES]
    q = _dot(cq, wuq_ref[...])
    kv = _dot(ckv, wukv_ref[...])
    cm, sm, cg, sg = cm_ref[...], sm_ref[...], cg_ref[...], sg_ref[...]
    qhg, khg, gqg, gkg = qhg_ref[...], khg_ref[...], gqg_ref[...], gkg_ref[...]
    ones_bd = _per_head_ones()
    krope2 = jnp.concatenate([krope, krope], axis=1)
    for pr in range(MLA_HEADS // 2):
        sl = slice(2 * pr * LANES, 2 * (pr + 1) * LANES)
        qm_ref[0, :, sl] = _head_pair_norm_rope(q[:, sl], qhg, cm, sm, MLA_QK, ones_bd)
        km_ref[0, :, sl] = _head_pair_norm_rope(kv[:, sl] + krope2, khg, cm, sm, MLA_QK, ones_bd)
    vm_ref[0] = kv[:, MLA_HEADS * LANES:].astype(BF16)
    for pr in range(GQA_HEADS // 2):
        sl = slice(2 * pr * LANES, 2 * (pr + 1) * LANES)
        src = proj[:, _P_GQ + sl.start:_P_GQ + sl.stop]
        qg_ref[0, :, sl] = _head_pair_norm_rope(src, gqg, cg, sg, GQA_HEAD_DIM, ones_bd)
    for pr in range(GQA_KV_HEADS // 2):
        sl = slice(2 * pr * LANES, 2 * (pr + 1) * LANES)
        src = proj[:, _P_GK + sl.start:_P_GK + sl.stop]
        kg_ref[0, :, sl] = _head_pair_norm_rope(src, gkg, cg, sg, GQA_HEAD_DIM, ones_bd)
    vg_ref[0] = proj[:, _P_GV:_P_GV + GQA_KV_HEADS * GQA_HEAD_DIM].astype(BF16)


def _const_spec(shape):
    nd = len(shape)
    return pl.BlockSpec(shape, lambda *_: (0,) * nd)


def _prep(x, g, win, qng, wuq, kvng, wukv, qhg, khg, gqg, gkg, cm, sm, cg, sg):
    b, s, d = x.shape
    t = TOKEN_TILE
    tok = lambda w: pl.BlockSpec((1, t, w), lambda bi, i: (bi, i, 0))
    tab = pl.BlockSpec((t, LANES), lambda bi, i: (i, 0))
    consts = (g, win, qng, wuq, kvng, wukv, qhg, khg, gqg, gkg)
    out_w = (MLA_HEADS * LANES, MLA_HEADS * LANES, MLA_HEADS * MLA_V,
             GQA_HEADS * LANES, GQA_KV_HEADS * LANES, GQA_KV_HEADS * GQA_HEAD_DIM)
    return pl.pallas_call(
        _prep_body,
        grid=(b, s // t),
        in_specs=[tok(d)] + [_const_spec(c.shape) for c in consts] + [tab] * 4,
        out_specs=[tok(w) for w in out_w],
        out_shape=[jax.ShapeDtypeStruct((b, s, w), BF16) for w in out_w],
        compiler_params=pltpu.CompilerParams(
            dimension_semantics=("parallel", "parallel"), vmem_limit_bytes=VMEM_LIMIT),
        name="prep",
    )(x, *consts, cm, sm, cg, sg)


def _flash(q, k_ref, k_off, vt_ref, seq_len, exp2_scale, track_max):
    rows = q.shape[0]
    nt_dims = (((1,), (1,)), ((), ()))

    def chunk(j):
        start = pl.multiple_of(j * KV_CHUNK, KV_CHUNK)
        kc = k_ref[0, pl.ds(start, KV_CHUNK), k_off:k_off + LANES]
        return lax.dot_general(kc, q, nt_dims, preferred_element_type=F32), vt_ref[j]

    def step_plain(j, carry):
        l, acc = carry
        st, vt = chunk(j)
        p = jnp.exp2(st * exp2_scale)
        return l + jnp.sum(p, axis=0, keepdims=True), acc + _dot(vt, p.astype(BF16))

    def step_max(j, carry):
        m, l, acc = carry
        st, vt = chunk(j)
        m_new = jnp.maximum(m, jnp.max(st, axis=0, keepdims=True))
        alpha = jnp.exp2((m - m_new) * exp2_scale)
        p = jnp.exp2((st - m_new) * exp2_scale)
        return m_new, alpha * l + jnp.sum(p, axis=0, keepdims=True), alpha * acc + _dot(vt, p.astype(BF16))

    l0 = jnp.zeros((1, rows), F32)
    acc0 = jnp.zeros((HEAD_V, rows), F32)
    n_chunks = seq_len // KV_CHUNK
    if track_max:
        _, l, acc = lax.fori_loop(0, n_chunks, step_max, (jnp.full((1, rows), -1e30, F32), l0, acc0))
    else:
        l, acc = lax.fori_loop(0, n_chunks, step_plain, (l0, acc0), unroll=True)
    return acc / l


def _pair_heads(ot_even, ot_odd):
    return jnp.concatenate([ot_even, ot_odd], axis=0).T


def _attn_mla_body(q_ref, k_ref, vt_ref, o_ref, *, seq_len, exp2_scale, track_max):
    o0 = _flash(q_ref[0, :, 0:LANES], k_ref, 0, vt_ref.at[0, 0], seq_len, exp2_scale, track_max)
    o1 = _flash(q_ref[0, :, LANES:2 * LANES], k_ref, LANES, vt_ref.at[0, 1], seq_len, exp2_scale, track_max)
    o_ref[0] = _pair_heads(o0, o1).astype(o_ref.dtype)


def _attn_gqa_body(q_ref, k_ref, vt_ref, o_ref, *, seq_len, exp2_scale, track_max):
    rep = GQA_HEADS // GQA_KV_HEADS
    tq = q_ref.shape[1]
    q = jnp.concatenate([q_ref[0, :, i * LANES:(i + 1) * LANES] for i in range(rep)], axis=0)
    ot = _flash(q, k_ref, 0, vt_ref.at[0, 0], seq_len, exp2_scale, track_max)
    for j in range(rep // 2):
        pair = _pair_heads(ot[:, (2 * j) * tq:(2 * j + 1) * tq], ot[:, (2 * j + 1) * tq:(2 * j + 2) * tq])
        o_ref[0, :, j * LANES:(j + 1) * LANES] = pair.astype(o_ref.dtype)


def _values_transposed(v, heads):
    b, s, _ = v.shape
    return v.reshape(b, s // KV_CHUNK, KV_CHUNK, heads, HEAD_V).transpose(0, 3, 1, 4, 2)


def _vt_spec(heads_per_step, n_chunks):
    return pl.BlockSpec((1, heads_per_step, n_chunks, HEAD_V, KV_CHUNK), lambda bi, h, qi: (bi, h, 0, 0, 0))


def _attn_mla(q, k, v, *, track_max):
    b, s, _ = q.shape
    tq = Q_TILE_MLA
    body = functools.partial(_attn_mla_body, seq_len=s, exp2_scale=MLA_QK ** -0.5 * math.log2(math.e),
                             track_max=track_max)
    return pl.pallas_call(
        body,
        grid=(b, MLA_HEADS // 2, s // tq),
        in_specs=[pl.BlockSpec((1, tq, 2 * LANES), lambda bi, hp, qi: (bi, qi, hp)),
                  pl.BlockSpec((1, s, 2 * LANES), lambda bi, hp, qi: (bi, 0, hp)),
                  _vt_spec(2, s // KV_CHUNK)],
        out_specs=pl.BlockSpec((1, tq, LANES), lambda bi, hp, qi: (bi, qi, hp)),
        out_shape=jax.ShapeDtypeStruct((b, s, MLA_HEADS * MLA_V), BF16),
        compiler_params=pltpu.CompilerParams(
            dimension_semantics=("parallel", "parallel", "arbitrary"), vmem_limit_bytes=VMEM_LIMIT),
        name="attn_mla_max" if track_max else "attn_mla",
    )(q, k, v)


def _attention(attn, q, k, v, head_dim, q_gain, k_gain):
    bound = math.sqrt(head_dim) * jnp.max(jnp.abs(q_gain)) * jnp.max(jnp.abs(k_gain))
    return lax.cond(bound <= MAX_SHIFT_FREE_SCORE,
                    functools.partial(attn, track_max=False), functools.partial(attn, track_max=True), q, k, v)


def _attn_gqa(q, k, v, *, track_max):
    b, s, _ = q.shape
    tq = Q_TILE_GQA
    rep = GQA_HEADS // GQA_KV_HEADS
    body = functools.partial(_attn_gqa_body, seq_len=s, exp2_scale=GQA_HEAD_DIM ** -0.5 * math.log2(math.e),
                             track_max=track_max)
    return pl.pallas_call(
        body,
        grid=(b, GQA_KV_HEADS, s // tq),
        in_specs=[pl.BlockSpec((1, tq, rep * LANES), lambda bi, g, qi: (bi, qi, g)),
                  pl.BlockSpec((1, s, LANES), lambda bi, g, qi: (bi, 0, g)),
                  _vt_spec(1, s // KV_CHUNK)],
        out_specs=pl.BlockSpec((1, tq, rep * GQA_HEAD_DIM), lambda bi, g, qi: (bi, qi, g)),
        out_shape=jax.ShapeDtypeStruct((b, s, GQA_HEADS * GQA_HEAD_DIM), BF16),
        compiler_params=pltpu.CompilerParams(
            dimension_semantics=("parallel", "parallel", "arbitrary"), vmem_limit_bytes=VMEM_LIMIT),
        name="attn_gqa_max" if track_max else "attn_gqa",
    )(q, k, v)


def _fill_normed_ext(h_scr, x, xp_ref, xn_ref, g):
    i = pl.program_id(1)
    t = x.shape[0]
    hp = _rms(xp_ref[0], g)
    hn = _rms(xn_ref[0], g)
    h_scr[0:HALO] = jnp.where(i > 0, hp, 0.0).astype(BF16)
    h_scr[HALO + t:] = jnp.where(i < pl.num_programs(1) - 1, hn, 0.0).astype(BF16)
    h_scr[HALO:HALO + t] = _rms(x, g).astype(BF16)


def _halo_specs(t, d):
    per = t // HALO
    main = pl.BlockSpec((1, t, d), lambda bi, i: (bi, i, 0))
    prev = pl.BlockSpec((1, HALO, d), lambda bi, i: (bi, jnp.maximum(i * per - 1, 0), 0))
    nxt = lambda n_blocks: pl.BlockSpec(
        (1, HALO, d), lambda bi, i: (bi, jnp.minimum((i + 1) * per, n_blocks - 1), 0))
    return main, prev, nxt


def _merge_body(x_ref, xp_ref, xn_ref, om_ref, og_ref, g_ref, wgate_ref, wpool_ref, poolw_ref, pscale_ref,
                wbm_ref, wbg_ref, wbp_ref, wout_ref, o_ref, h_scr, p_scr, *, seq_len):
    x = x_ref[0]
    t = x.shape[0]
    _fill_normed_ext(h_scr, x, xp_ref, xn_ref, g_ref[...])
    h = h_scr[HALO:HALO + t]
    p_scr[...] = _dot(h_scr[...], wpool_ref[...])
    pos = pl.program_id(1) * t + lax.broadcasted_iota(jnp.int32, (t, 1), 0)
    ys = []
    for gi, w in enumerate(POOL_WINDOWS):
        sl = slice(gi * POOL_GROUP, (gi + 1) * POOL_GROUP)
        tot = None
        for d in range(-(w // 2), w - w // 2):
            part = p_scr[HALO + d:HALO + d + t, sl]
            tot = part if tot is None else tot + part
        cnt = jnp.minimum(pos + (w - w // 2), seq_len) - jnp.maximum(pos - w // 2, 0)
        mixed = tot / cnt.astype(F32) - p_scr[HALO:HALO + t, sl]
        ys.append(_dot(mixed.astype(BF16), poolw_ref[gi]))
    y = (jnp.concatenate(ys, axis=1) * pscale_ref[...]).astype(BF16)
    gates = jax.nn.sigmoid(_dot(h, wgate_ref[...]))
    merged = (gates[:, 0:D_MODEL] * _dot(om_ref[0], wbm_ref[...])
              + gates[:, D_MODEL:2 * D_MODEL] * _dot(og_ref[0], wbg_ref[...])
              + gates[:, 2 * D_MODEL:3 * D_MODEL] * _dot(y, wbp_ref[...]))
    o_ref[0] = x + _dot(merged.astype(BF16), wout_ref[...])


def _merge(x, om, og, g, wgate, wpool, poolw, pscale, wbm, wbg, wbp, wout):
    b, s, d = x.shape
    t = TOKEN_TILE
    main, prev, nxt = _halo_specs(t, d)
    tok = lambda w: pl.BlockSpec((1, t, w), lambda bi, i: (bi, i, 0))
    consts = (g, wgate, wpool, poolw, pscale, wbm, wbg, wbp, wout)
    return pl.pallas_call(
        functools.partial(_merge_body, seq_len=s),
        grid=(b, s // t),
        in_specs=[main, prev, nxt(s // HALO), tok(om.shape[-1]), tok(og.shape[-1])]
                 + [_const_spec(c.shape) for c in consts],
        out_specs=main,
        out_shape=jax.ShapeDtypeStruct(x.shape, x.dtype),
        scratch_shapes=[pltpu.VMEM((t + 2 * HALO, d), BF16), pltpu.VMEM((t + 2 * HALO, POOL_WIDTH), F32)],
        compiler_params=pltpu.CompilerParams(
            dimension_semantics=("parallel", "parallel"), vmem_limit_bytes=VMEM_LIMIT),
        name="merge",
    )(x, x, x, om, og, *consts)


def _ffn_body(x_ref, xp_ref, xn_ref, g_ref, wup_ref, cw_ref, wdn_ref, o_ref, h_scr, u_scr, act_scr):
    x = x_ref[0]
    t = x.shape[0]
    _fill_normed_ext(h_scr, x, xp_ref, xn_ref, g_ref[...])

    def conv_half(k, u):
        u[...] = _dot(h_scr[...], wup_ref[k])
        cw = cw_ref[k]
        return (cw[0:1] * u[HALO - 1:HALO - 1 + t] + cw[1:2] * u[HALO:HALO + t]
                + cw[2:3] * u[HALO + 1:HALO + 1 + t] + cw[3:4])

    for c in range(N_FF_CHUNKS):
        u_gate = conv_half(c, u_scr.at[2 * (c % 2)])
        u_val = conv_half(c + N_FF_CHUNKS, u_scr.at[2 * (c % 2) + 1])
        act_scr[:, c * FF_CHUNK:(c + 1) * FF_CHUNK] = (jax.nn.silu(u_gate) * u_val).astype(BF16)
    o_ref[0] = x + _dot(act_scr[...], wdn_ref[...])


def _ffn(x, g, wup, cw, wdn):
    b, s, d = x.shape
    t = TOKEN_TILE
    main, prev, nxt = _halo_specs(t, d)
    consts = (g, wup, cw, wdn)
    return pl.pallas_call(
        _ffn_body,
        grid=(b, s // t),
        in_specs=[main, prev, nxt(s // HALO)] + [_const_spec(c.shape) for c in consts],
        out_specs=main,
        out_shape=jax.ShapeDtypeStruct(x.shape, x.dtype),
        scratch_shapes=[pltpu.VMEM((t + 2 * HALO, d), BF16), pltpu.VMEM((4, t + 2 * HALO, FF_CHUNK), F32),
                        pltpu.VMEM((t, D_FF), BF16)],
        compiler_params=pltpu.CompilerParams(
            dimension_semantics=("parallel", "parallel"), vmem_limit_bytes=VMEM_LIMIT),
        name="ffn",
    )(x, x, x, *consts)


def _pad_gain(g, perm):
    return _take_cols(g[None, :], perm)


def kernel(x, attn_norm_g, w_in, mla_q_norm_g, mla_w_uq, mla_kv_norm_g, mla_w_ukv, mla_q_head_g, mla_k_head_g, gqa_q_head_g, gqa_k_head_g, pool_w, pool_scale, w_branch_mla, w_branch_gqa, w_branch_pool, w_out, ffn_norm_g, ffn_w_up, ffn_conv_w, ffn_conv_b, ffn_w_down):
    b, s, d = x.shape
    assert d == D_MODEL and s % GRID_W == 0
    assert all(s % t == 0 for t in (TOKEN_TILE, Q_TILE_MLA, Q_TILE_GQA, KV_CHUNK)), s
    depth = w_in.shape[0]
    mla_perm, gqa_perm = _mla_head_perm(), _gqa_head_perm()
    cm, sm = _rope_tables(s, MLA_ROPE, mla_perm, MLA_NOPE)
    cg, sg = _rope_tables(s, GQA_HEAD_DIM, gqa_perm, 0)
    prep_cols, uq_cols, ukv_cols = _prep_in_cols(), _uq_cols(), _ukv_cols()
    row = lambda v: v[None, :].astype(F32)
    for l in range(depth):
        qm, km, vm, qg, kg, vg = _prep(
            x, row(attn_norm_g[l]), _take_cols(w_in[l], prep_cols).astype(BF16),
            row(mla_q_norm_g[l]), _take_cols(mla_w_uq[l], uq_cols).astype(BF16),
            row(mla_kv_norm_g[l]), _take_cols(mla_w_ukv[l], ukv_cols).astype(BF16),
            _pad_gain(mla_q_head_g[l], mla_perm), _pad_gain(mla_k_head_g[l], mla_perm),
            _pad_gain(gqa_q_head_g[l], gqa_perm), _pad_gain(gqa_k_head_g[l], gqa_perm),
            cm, sm, cg, sg)
        o_mla = _attention(_attn_mla, qm, km, _values_transposed(vm, MLA_HEADS), MLA_QK,
                           mla_q_head_g[l], mla_k_head_g[l])
        o_gqa = _attention(_attn_gqa, qg, kg, _values_transposed(vg, GQA_KV_HEADS), GQA_HEAD_DIM,
                           gqa_q_head_g[l], gqa_k_head_g[l])
        x = _merge(
            x, o_mla, o_gqa, row(attn_norm_g[l]),
            w_in[l][:, _OFF_GATE:].astype(BF16), w_in[l][:, _OFF_POOL:_OFF_GATE].astype(BF16),
            pool_w[l].astype(BF16), row(pool_scale[l]),
            w_branch_mla[l].astype(BF16), w_branch_gqa[l].astype(BF16), w_branch_pool[l].astype(BF16),
            w_out[l].astype(BF16))
        wup = ffn_w_up[l].astype(BF16).reshape(d, 2 * N_FF_CHUNKS, FF_CHUNK).transpose(1, 0, 2)
        cw = jnp.concatenate([ffn_conv_w[l], ffn_conv_b[l][None, :], jnp.zeros((4, 2 * D_FF), F32)], axis=0)
        cw = cw.reshape(8, 2 * N_FF_CHUNKS, FF_CHUNK).transpose(1, 0, 2)
        wdn = ffn_w_down[l].astype(BF16)
        x = _ffn(x, row(ffn_norm_g[l]), wup, cw, wdn)
    return x
```

```python
import functools
import math

import numpy as np
import jax
import jax.numpy as jnp
from jax import lax
from jax.experimental import pallas as pl
from jax.experimental.pallas import tpu as pltpu

F32 = jnp.float32
BF16 = jnp.bfloat16

D_MODEL = 1024
GRID_W = 64
ROPE_THETA = 10000.0
EPS = 1e-6
MLA_HEADS = 8
MLA_Q_LORA = 256
MLA_KV_LORA = 128
MLA_NOPE = 64
MLA_ROPE = 32
MLA_V = 64
MLA_QK = MLA_NOPE + MLA_ROPE
GQA_HEADS = 8
GQA_KV_HEADS = 2
GQA_HEAD_DIM = 64
POOL_WINDOWS = (2, 4, 8, 16)
POOL_GROUPS = 4
POOL_GROUP = 128
POOL_WIDTH = POOL_GROUPS * POOL_GROUP
N_BRANCH = 3
D_FF = 2816

LANES = 128
ROPE_LANE_SHIFT = 64
HALO = 16
HEAD_V = 64
DENOM_LANE = HEAD_V
VMEM_LIMIT = 56 * 1024 * 1024

TOKEN_TILE = 512
Q_TILE_MLA = 1024
Q_TILE_GQA = 512
KV_CHUNK = 1024
FF_CHUNK = 256
N_FF_CHUNKS = D_FF // FF_CHUNK

MAX_SHIFT_FREE_SCORE = 60.0

_OFF_CQ = 0
_OFF_CKV = _OFF_CQ + MLA_Q_LORA
_OFF_KROPE = _OFF_CKV + MLA_KV_LORA
_OFF_GQ = _OFF_KROPE + MLA_ROPE
_OFF_GK = _OFF_GQ + GQA_HEADS * GQA_HEAD_DIM
_OFF_GV = _OFF_GK + GQA_KV_HEADS * GQA_HEAD_DIM
_OFF_POOL = _OFF_GV + GQA_KV_HEADS * GQA_HEAD_DIM
_OFF_GATE = _OFF_POOL + POOL_WIDTH


def _mla_head_perm():
    perm = np.full((LANES,), -1, np.int64)
    half = MLA_ROPE // 2
    perm[0:half] = MLA_NOPE + np.arange(half)
    perm[ROPE_LANE_SHIFT:ROPE_LANE_SHIFT + half] = MLA_NOPE + half + np.arange(half)
    perm[half:ROPE_LANE_SHIFT] = np.arange(ROPE_LANE_SHIFT - half)
    rest = MLA_NOPE - (ROPE_LANE_SHIFT - half)
    perm[ROPE_LANE_SHIFT + half:ROPE_LANE_SHIFT + half + rest] = (ROPE_LANE_SHIFT - half) + np.arange(rest)
    return perm


def _gqa_head_perm():
    perm = np.full((LANES,), -1, np.int64)
    half = GQA_HEAD_DIM // 2
    perm[0:half] = np.arange(half)
    perm[ROPE_LANE_SHIFT:ROPE_LANE_SHIFT + half] = half + np.arange(half)
    return perm


def _take_cols(w, idx):
    idx = np.asarray(idx)
    cols = jnp.take(w, jnp.asarray(np.maximum(idx, 0)), axis=-1)
    return jnp.where(jnp.asarray(idx >= 0), cols, jnp.zeros((), w.dtype))


def _prep_in_cols():
    mla = _mla_head_perm()
    gqa = _gqa_head_perm()
    cols = [np.arange(_OFF_CQ, _OFF_CQ + MLA_Q_LORA), np.arange(_OFF_CKV, _OFF_CKV + MLA_KV_LORA)]
    kr = np.where(mla >= MLA_NOPE, _OFF_KROPE + (mla - MLA_NOPE), -1)
    cols.append(kr)
    for h in range(GQA_HEADS):
        cols.append(np.where(gqa >= 0, _OFF_GQ + h * GQA_HEAD_DIM + gqa, -1))
    for g in range(GQA_KV_HEADS):
        cols.append(np.where(gqa >= 0, _OFF_GK + g * GQA_HEAD_DIM + gqa, -1))
    for g in range(GQA_KV_HEADS):
        v = np.full((LANES,), -1, np.int64)
        v[:GQA_HEAD_DIM] = _OFF_GV + g * GQA_HEAD_DIM + np.arange(GQA_HEAD_DIM)
        cols.append(v)
    return np.concatenate(cols)


_PREP_W = MLA_Q_LORA + MLA_KV_LORA + LANES + (GQA_HEADS + 2 * GQA_KV_HEADS) * LANES
_P_CQ = 0
_P_CKV = MLA_Q_LORA
_P_KROPE = _P_CKV + MLA_KV_LORA
_P_GQ = _P_KROPE + LANES
_P_GK = _P_GQ + GQA_HEADS * LANES
_P_GV = _P_GK + GQA_KV_HEADS * LANES


def _uq_cols():
    mla = _mla_head_perm()
    return np.concatenate([np.where(mla >= 0, h * MLA_QK + mla, -1) for h in range(MLA_HEADS)])


def _ukv_cols():
    mla = _mla_head_perm()
    per_head = MLA_NOPE + MLA_V
    k = [np.where((mla >= 0) & (mla < MLA_NOPE), h * per_head + mla, -1) for h in range(MLA_HEADS)]
    v = []
    for h in range(MLA_HEADS):
        vh = np.full((LANES,), -1, np.int64)
        vh[:MLA_V] = h * per_head + MLA_NOPE + np.arange(MLA_V)
        v.append(vh)
    return np.concatenate(k + v)


def _rope_tables(seq_len, rot_dim, perm, first_rope_dim):
    rows = seq_len // GRID_W
    row_idx = jnp.repeat(jnp.arange(rows, dtype=F32), GRID_W)
    col_idx = jnp.tile(jnp.arange(GRID_W, dtype=F32), rows)
    n_axis = rot_dim // 4
    inv_freq = ROPE_THETA ** (-jnp.arange(n_axis, dtype=F32) / n_axis)
    ang = jnp.concatenate([row_idx[:, None] * inv_freq, col_idx[:, None] * inv_freq], axis=-1)
    cos, sin = jnp.cos(ang), jnp.sin(ang)
    half = rot_dim // 2
    rel = perm - first_rope_dim
    is_rope = (perm >= first_rope_dim)
    freq = np.where(is_rope, rel % half, 0)
    sign = np.where(is_rope, np.where(rel < half, -1.0, 1.0), 0.0).astype(np.float32)
    c = jnp.where(jnp.asarray(is_rope), jnp.take(cos, jnp.asarray(freq), axis=1), 1.0)
    s = jnp.take(sin, jnp.asarray(freq), axis=1) * jnp.asarray(sign)
    return c.astype(F32), s.astype(F32)


def _rms(x, g):
    return x * lax.rsqrt(jnp.mean(x * x, axis=-1, keepdims=True) + EPS) * g


def _dot(a, b):
    return jnp.dot(a, b, preferred_element_type=F32)


def _per_head_ones():
    head = lambda axis: lax.broadcasted_iota(jnp.int32, (2 * LANES, 2 * LANES), axis) // LANES
    return jnp.where(head(0) == head(1), 1.0, 0.0).astype(BF16)


def _head_pair_norm_rope(xp, g, cos, sin, dim, ones_bd):
    sq = xp * xp
    hi = sq.astype(BF16)
    lo = (sq - hi.astype(F32)).astype(BF16)
    ms = (_dot(hi, ones_bd) + _dot(lo, ones_bd)) * (1.0 / dim)
    outs = []
    for i in range(2):
        sl = slice(i * LANES, (i + 1) * LANES)
        xn = xp[:, sl] * lax.rsqrt(ms[:, sl] + EPS) * g
        outs.append((xn * cos + pltpu.roll(xn, ROPE_LANE_SHIFT, 1) * sin).astype(BF16))
    return jnp.concatenate(outs, axis=1)


def _with_denominator_column(v):
    lane = lax.broadcasted_iota(jnp.int32, v.shape, 1)
    return jnp.where((lane & (LANES - 1)) == DENOM_LANE, 1.0, v)


def _prep_body(x_ref, g_ref, win_ref, qng_ref, wuq_ref, kvng_ref, wukv_ref, qhg_ref, khg_ref,
               gqg_ref, gkg_ref, cm_ref, sm_ref, cg_ref, sg_ref,
               qm_ref, km_ref, vm_ref, qg_ref, kg_ref, vg_ref):
    h = _rms(x_ref[0], g_ref[...]).astype(BF16)
    proj = _dot(h, win_ref[...])
    cq = _rms(proj[:, _P_CQ:_P_CQ + MLA_Q_LORA], qng_ref[...]).astype(BF16)
    ckv = _rms(proj[:, _P_CKV:_P_CKV + MLA_KV_LORA], kvng_ref[...]).astype(BF16)
    krope = proj[:, _P_KROPE:_P_KROPE + LANES]
    q = _dot(cq, wuq_ref[...])
    kv = _dot(ckv, wukv_ref[...])
    cm, sm, cg, sg = cm_ref[...], sm_ref[...], cg_ref[...], sg_ref[...]
    qhg, khg, gqg, gkg = qhg_ref[...], khg_ref[...], gqg_ref[...], gkg_ref[...]
    ones_bd = _per_head_ones()
    krope2 = jnp.concatenate([krope, krope], axis=1)
    for pr in range(MLA_HEADS // 2):
        sl = slice(2 * pr * LANES, 2 * (pr + 1) * LANES)
        qm_ref[0, :, sl] = _head_pair_norm_rope(q[:, sl], qhg, cm, sm, MLA_QK, ones_bd)
        km_ref[0, :, sl] = _head_pair_norm_rope(kv[:, sl] + krope2, khg, cm, sm, MLA_QK, ones_bd)
    vm_ref[0] = _with_denominator_column(kv[:, MLA_HEADS * LANES:]).astype(BF16)
    for pr in range(GQA_HEADS // 2):
        sl = slice(2 * pr * LANES, 2 * (pr + 1) * LANES)
        src = proj[:, _P_GQ + sl.start:_P_GQ + sl.stop]
        qg_ref[0, :, sl] = _head_pair_norm_rope(src, gqg, cg, sg, GQA_HEAD_DIM, ones_bd)
    for pr in range(GQA_KV_HEADS // 2):
        sl = slice(2 * pr * LANES, 2 * (pr + 1) * LANES)
        src = proj[:, _P_GK + sl.start:_P_GK + sl.stop]
        kg_ref[0, :, sl] = _head_pair_norm_rope(src, gkg, cg, sg, GQA_HEAD_DIM, ones_bd)
    vg_ref[0] = _with_denominator_column(proj[:, _P_GV:_P_GV + GQA_KV_HEADS * LANES]).astype(BF16)


def _const_spec(shape):
    nd = len(shape)
    return pl.BlockSpec(shape, lambda *_: (0,) * nd)


def _prep(x, g, win, qng, wuq, kvng, wukv, qhg, khg, gqg, gkg, cm, sm, cg, sg):
    b, s, d = x.shape
    t = TOKEN_TILE
    tok = lambda w: pl.BlockSpec((1, t, w), lambda bi, i: (bi, i, 0))
    tab = pl.BlockSpec((t, LANES), lambda bi, i: (i, 0))
    consts = (g, win, qng, wuq, kvng, wukv, qhg, khg, gqg, gkg)
    out_w = (MLA_HEADS * LANES, MLA_HEADS * LANES, MLA_HEADS * LANES,
             GQA_HEADS * LANES, GQA_KV_HEADS * LANES, GQA_KV_HEADS * LANES)
    return pl.pallas_call(
        _prep_body,
        grid=(b, s // t),
        in_specs=[tok(d)] + [_const_spec(c.shape) for c in consts] + [tab] * 4,
        out_specs=[tok(w) for w in out_w],
        out_shape=[jax.ShapeDtypeStruct((b, s, w), BF16) for w in out_w],
        compiler_params=pltpu.CompilerParams(
            dimension_semantics=("parallel", "parallel"), vmem_limit_bytes=VMEM_LIMIT),
        name="prep",
    )(x, *consts, cm, sm, cg, sg)


def _flash(q, k_ref, k_off, vt_ref, seq_len, exp2_scale, track_max):
    rows = q.shape[0]
    nt_dims = (((1,), (1,)), ((), ()))

    def chunk(j):
        start = pl.multiple_of(j * KV_CHUNK, KV_CHUNK)
        kc = k_ref[0, pl.ds(start, KV_CHUNK), k_off:k_off + LANES]
        return lax.dot_general(kc, q, nt_dims, preferred_element_type=F32), vt_ref[j]

    def step_plain(j, acc):
        st, vt = chunk(j)
        return acc + _dot(vt, jnp.exp2(st * exp2_scale).astype(BF16))

    def step_max(j, carry):
        m, acc = carry
        st, vt = chunk(j)
        m_new = jnp.maximum(m, jnp.max(st, axis=0, keepdims=True))
        alpha = jnp.exp2((m - m_new) * exp2_scale)
        p = jnp.exp2((st - m_new) * exp2_scale)
        return m_new, alpha * acc + _dot(vt, p.astype(BF16))

    acc0 = jnp.zeros((LANES, rows), F32)
    n_chunks = seq_len // KV_CHUNK
    if track_max:
        _, acc = lax.fori_loop(0, n_chunks, step_max, (jnp.full((1, rows), -1e30, F32), acc0))
    else:
        acc = lax.fori_loop(0, n_chunks, step_plain, acc0, unroll=True)
    return (acc / acc[DENOM_LANE:DENOM_LANE + 1, :]).T


def _pair_heads(o_even, o_odd):
    lane = lax.broadcasted_iota(jnp.int32, o_even.shape, 1)
    return jnp.where(lane < HEAD_V, o_even, pltpu.roll(o_odd, HEAD_V, 1))


def _attn_mla_body(q_ref, k_ref, vt_ref, o_ref, *, seq_len, exp2_scale, track_max):
    o0 = _flash(q_ref[0, :, 0:LANES], k_ref, 0, vt_ref.at[0, 0], seq_len, exp2_scale, track_max)
    o1 = _flash(q_ref[0, :, LANES:2 * LANES], k_ref, LANES, vt_ref.at[0, 1], seq_len, exp2_scale, track_max)
    o_ref[0] = _pair_heads(o0, o1).astype(o_ref.dtype)


def _attn_gqa_body(q_ref, k_ref, vt_ref, o_ref, *, seq_len, exp2_scale, track_max):
    rep = GQA_HEADS // GQA_KV_HEADS
    tq = q_ref.shape[1]
    q = jnp.concatenate([q_ref[0, :, i * LANES:(i + 1) * LANES] for i in range(rep)], axis=0)
    o = _flash(q, k_ref, 0, vt_ref.at[0, 0], seq_len, exp2_scale, track_max)
    for j in range(rep // 2):
        pair = _pair_heads(o[(2 * j) * tq:(2 * j + 1) * tq], o[(2 * j + 1) * tq:(2 * j + 2) * tq])
        o_ref[0, :, j * LANES:(j + 1) * LANES] = pair.astype(o_ref.dtype)


def _values_transposed(v, heads):
    b, s, _ = v.shape
    return v.reshape(b, s // KV_CHUNK, KV_CHUNK, heads, LANES).transpose(0, 3, 1, 4, 2)


def _vt_spec(heads_per_step, n_chunks):
    return pl.BlockSpec((1, heads_per_step, n_chunks, LANES, KV_CHUNK), lambda bi, h, qi: (bi, h, 0, 0, 0))


def _attn_mla(q, k, v, *, track_max):
    b, s, _ = q.shape
    tq = Q_TILE_MLA
    body = functools.partial(_attn_mla_body, seq_len=s, exp2_scale=MLA_QK ** -0.5 * math.log2(math.e),
                             track_max=track_max)
    return pl.pallas_call(
        body,
        grid=(b, MLA_HEADS // 2, s // tq),
        in_specs=[pl.BlockSpec((1, tq, 2 * LANES), lambda bi, hp, qi: (bi, qi, hp)),
                  pl.BlockSpec((1, s, 2 * LANES), lambda bi, hp, qi: (bi, 0, hp)),
                  _vt_spec(2, s // KV_CHUNK)],
        out_specs=pl.BlockSpec((1, tq, LANES), lambda bi, hp, qi: (bi, qi, hp)),
        out_shape=jax.ShapeDtypeStruct((b, s, MLA_HEADS * MLA_V), BF16),
        compiler_params=pltpu.CompilerParams(
            dimension_semantics=("parallel", "parallel", "arbitrary"), vmem_limit_bytes=VMEM_LIMIT),
        name="attn_mla_max" if track_max else "attn_mla",
    )(q, k, v)


def _attention(attn, q, k, v, head_dim, q_gain, k_gain):
    bound = math.sqrt(head_dim) * jnp.max(jnp.abs(q_gain)) * jnp.max(jnp.abs(k_gain))
    return lax.cond(bound <= MAX_SHIFT_FREE_SCORE,
                    functools.partial(attn, track_max=False), functools.partial(attn, track_max=True), q, k, v)


def _attn_gqa(q, k, v, *, track_max):
    b, s, _ = q.shape
    tq = Q_TILE_GQA
    rep = GQA_HEADS // GQA_KV_HEADS
    body = functools.partial(_attn_gqa_body, seq_len=s, exp2_scale=GQA_HEAD_DIM ** -0.5 * math.log2(math.e),
                             track_max=track_max)
    return pl.pallas_call(
        body,
        grid=(b, GQA_KV_HEADS, s // tq),
        in_specs=[pl.BlockSpec((1, tq, rep * LANES), lambda bi, g, qi: (bi, qi, g)),
                  pl.BlockSpec((1, s, LANES), lambda bi, g, qi: (bi, 0, g)),
                  _vt_spec(1, s // KV_CHUNK)],
        out_specs=pl.BlockSpec((1, tq, rep * GQA_HEAD_DIM), lambda bi, g, qi: (bi, qi, g)),
        out_shape=jax.ShapeDtypeStruct((b, s, GQA_HEADS * GQA_HEAD_DIM), BF16),
        compiler_params=pltpu.CompilerParams(
            dimension_semantics=("parallel", "parallel", "arbitrary"), vmem_limit_bytes=VMEM_LIMIT),
        name="attn_gqa_max" if track_max else "attn_gqa",
    )(q, k, v)


def _fill_normed_ext(h_scr, x, xp_ref, xn_ref, g):
    i = pl.program_id(1)
    t = x.shape[0]
    h_scr[HALO:HALO + t] = _rms(x, g).astype(BF16)
    hp = _rms(xp_ref[0], g)
    hn = _rms(xn_ref[0], g)
    h_scr[0:HALO] = jnp.where(i > 0, hp, 0.0).astype(BF16)
    h_scr[HALO + t:] = jnp.where(i < pl.num_programs(1) - 1, hn, 0.0).astype(BF16)


def _halo_specs(t, d):
    per = t // HALO
    main = pl.BlockSpec((1, t, d), lambda bi, i: (bi, i, 0))
    prev = pl.BlockSpec((1, HALO, d), lambda bi, i: (bi, jnp.maximum(i * per - 1, 0), 0))
    nxt = lambda n_blocks: pl.BlockSpec(
        (1, HALO, d), lambda bi, i: (bi, jnp.minimum((i + 1) * per, n_blocks - 1), 0))
    return main, prev, nxt


def _merge_body(x_ref, xp_ref, xn_ref, om_ref, og_ref, g_ref, wgate_ref, wpool_ref, poolw_ref, pscale_ref,
                wbm_ref, wbg_ref, wbp_ref, wout_ref, o_ref, h_scr, p_scr, *, seq_len):
    x = x_ref[0]
    t = x.shape[0]
    _fill_normed_ext(h_scr, x, xp_ref, xn_ref, g_ref[...])
    h = h_scr[HALO:HALO + t]
    p_scr[...] = _dot(h_scr[...], wpool_ref[...])
    pos = pl.program_id(1) * t + lax.broadcasted_iota(jnp.int32, (t, 1), 0)
    ys = []
    for gi, w in enumerate(POOL_WINDOWS):
        sl = slice(gi * POOL_GROUP, (gi + 1) * POOL_GROUP)
        tot = None
        for d in range(-(w // 2), w - w // 2):
            part = p_scr[HALO + d:HALO + d + t, sl]
            tot = part if tot is None else tot + part
        cnt = jnp.minimum(pos + (w - w // 2), seq_len) - jnp.maximum(pos - w // 2, 0)
        mixed = tot / cnt.astype(F32) - p_scr[HALO:HALO + t, sl]
        ys.append(_dot(mixed.astype(BF16), poolw_ref[gi]))
    y = (jnp.concatenate(ys, axis=1) * pscale_ref[...]).astype(BF16)
    gates = jax.nn.sigmoid(_dot(h, wgate_ref[...]))
    merged = (gates[:, 0:D_MODEL] * _dot(om_ref[0], wbm_ref[...])
              + gates[:, D_MODEL:2 * D_MODEL] * _dot(og_ref[0], wbg_ref[...])
              + gates[:, 2 * D_MODEL:3 * D_MODEL] * _dot(y, wbp_ref[...]))
    o_ref[0] = x + _dot(merged.astype(BF16), wout_ref[...])


def _merge(x, om, og, g, wgate, wpool, poolw, pscale, wbm, wbg, wbp, wout):
    b, s, d = x.shape
    t = TOKEN_TILE
    main, prev, nxt = _halo_specs(t, d)
    tok = lambda w: pl.BlockSpec((1, t, w), lambda bi, i: (bi, i, 0))
    consts = (g, wgate, wpool, poolw, pscale, wbm, wbg, wbp, wout)
    return pl.pallas_call(
        functools.partial(_merge_body, seq_len=s),
        grid=(b, s // t),
        in_specs=[main, prev, nxt(s // HALO), tok(om.shape[-1]), tok(og.shape[-1])]
                 + [_const_spec(c.shape) for c in consts],
        out_specs=main,
        out_shape=jax.ShapeDtypeStruct(x.shape, x.dtype),
        scratch_shapes=[pltpu.VMEM((t + 2 * HALO, d), BF16), pltpu.VMEM((t + 2 * HALO, POOL_WIDTH), F32)],
        compiler_params=pltpu.CompilerParams(
            dimension_semantics=("parallel", "parallel"), vmem_limit_bytes=VMEM_LIMIT),
        name="merge",
    )(x, x, x, om, og, *consts)


def _ffn_body(x_ref, xp_ref, xn_ref, g_ref, wup_ref, cw_ref, wdn_ref, o_ref, h_scr, u_scr, act_scr):
    x = x_ref[0]
    t = x.shape[0]
    _fill_normed_ext(h_scr, x, xp_ref, xn_ref, g_ref[...])

    def conv_half(k, u):
        u[...] = _dot(h_scr[...], wup_ref[k])
        cw = cw_ref[k]
        return (cw[0:1] * u[HALO - 1:HALO - 1 + t] + cw[1:2] * u[HALO:HALO + t]
                + cw[2:3] * u[HALO + 1:HALO + 1 + t] + cw[3:4])

    for c in range(N_FF_CHUNKS):
        u_gate = conv_half(c, u_scr.at[2 * (c % 2)])
        u_val = conv_half(c + N_FF_CHUNKS, u_scr.at[2 * (c % 2) + 1])
        act_scr[:, c * FF_CHUNK:(c + 1) * FF_CHUNK] = (jax.nn.silu(u_gate) * u_val).astype(BF16)
    o_ref[0] = x + _dot(act_scr[...], wdn_ref[...])


def _ffn(x, g, wup, cw, wdn):
    b, s, d = x.shape
    t = TOKEN_TILE
    main, prev, nxt = _halo_specs(t, d)
    consts = (g, wup, cw, wdn)
    return pl.pallas_call(
        _ffn_body,
        grid=(b, s // t),
        in_specs=[main, prev, nxt(s // HALO)] + [_const_spec(c.shape) for c in consts],
        out_specs=main,
        out_shape=jax.ShapeDtypeStruct(x.shape, x.dtype),
        scratch_shapes=[pltpu.VMEM((t + 2 * HALO, d), BF16), pltpu.VMEM((4, t + 2 * HALO, FF_CHUNK), F32),
                        pltpu.VMEM((t, D_FF), BF16)],
        compiler_params=pltpu.CompilerParams(
            dimension_semantics=("parallel", "parallel"), vmem_limit_bytes=VMEM_LIMIT),
        name="ffn",
    )(x, x, x, *consts)


def _pad_gain(g, perm):
    return _take_cols(g[None, :], perm)


def kernel(x, attn_norm_g, w_in, mla_q_norm_g, mla_w_uq, mla_kv_norm_g, mla_w_ukv, mla_q_head_g, mla_k_head_g, gqa_q_head_g, gqa_k_head_g, pool_w, pool_scale, w_branch_mla, w_branch_gqa, w_branch_pool, w_out, ffn_norm_g, ffn_w_up, ffn_conv_w, ffn_conv_b, ffn_w_down):
    b, s, d = x.shape
    assert d == D_MODEL and s % GRID_W == 0
    assert all(s % t == 0 for t in (TOKEN_TILE, Q_TILE_MLA, Q_TILE_GQA, KV_CHUNK)), s
    depth = w_in.shape[0]
    mla_perm, gqa_perm = _mla_head_perm(), _gqa_head_perm()
    cm, sm = _rope_tables(s, MLA_ROPE, mla_perm, MLA_NOPE)
    cg, sg = _rope_tables(s, GQA_HEAD_DIM, gqa_perm, 0)
    prep_cols, uq_cols, ukv_cols = _prep_in_cols(), _uq_cols(), _ukv_cols()
    row = lambda v: v[None, :].astype(F32)
    for l in range(depth):
        qm, km, vm, qg, kg, vg = _prep(
            x, row(attn_norm_g[l]), _take_cols(w_in[l], prep_cols).astype(BF16),
            row(mla_q_norm_g[l]), _take_cols(mla_w_uq[l], uq_cols).astype(BF16),
            row(mla_kv_norm_g[l]), _take_cols(mla_w_ukv[l], ukv_cols).astype(BF16),
            _pad_gain(mla_q_head_g[l], mla_perm), _pad_gain(mla_k_head_g[l], mla_perm),
            _pad_gain(gqa_q_head_g[l], gqa_perm), _pad_gain(gqa_k_head_g[l], gqa_perm),
            cm, sm, cg, sg)
        o_mla = _attention(_attn_mla, qm, km, _values_transposed(vm, MLA_HEADS), MLA_QK,
                           mla_q_head_g[l], mla_k_head_g[l])
        o_gqa = _attention(_attn_gqa, qg, kg, _values_transposed(vg, GQA_KV_HEADS), GQA_HEAD_DIM,
                           gqa_q_head_g[l], gqa_k_head_g[l])
        x = _merge(
            x, o_mla, o_gqa, row(attn_norm_g[l]),
            w_in[l][:, _OFF_GATE:].astype(BF16), w_in[l][:, _OFF_POOL:_OFF_GATE].astype(BF16),
            pool_w[l].astype(BF16), row(pool_scale[l]),
            w_branch_mla[l].astype(BF16), w_branch_gqa[l].astype(BF16), w_branch_pool[l].astype(BF16),
            w_out[l].astype(BF16))
        wup = ffn_w_up[l].astype(BF16).reshape(d, 2 * N_FF_CHUNKS, FF_CHUNK).transpose(1, 0, 2)
        cw = jnp.concatenate([ffn_conv_w[l], ffn_conv_b[l][None, :], jnp.zeros((4, 2 * D_FF), F32)], axis=0)
        cw = cw.reshape(8, 2 * N_FF_CHUNKS, FF_CHUNK).transpose(1, 0, 2)
        wdn = ffn_w_down[l].astype(BF16)
        x = _ffn(x, row(ffn_norm_g[l]), wup, cw, wdn)
    return x
```

```python
import functools
import math

import numpy as np
import jax
import jax.numpy as jnp
from jax import lax
from jax.experimental import pallas as pl
from jax.experimental.pallas import tpu as pltpu

F32 = jnp.float32
BF16 = jnp.bfloat16

D_MODEL = 1024
GRID_W = 64
ROPE_THETA = 10000.0
EPS = 1e-6
MLA_HEADS = 8
MLA_Q_LORA = 256
MLA_KV_LORA = 128
MLA_NOPE = 64
MLA_ROPE = 32
MLA_V = 64
MLA_QK = MLA_NOPE + MLA_ROPE
GQA_HEADS = 8
GQA_KV_HEADS = 2
GQA_HEAD_DIM = 64
POOL_WINDOWS = (2, 4, 8, 16)
POOL_GROUPS = 4
POOL_GROUP = 128
POOL_WIDTH = POOL_GROUPS * POOL_GROUP
N_BRANCH = 3
D_FF = 2816

LANES = 128
ROPE_LANE_SHIFT = 64
HALO = 16
HEAD_V = 64
DENOM_LANE = HEAD_V
VMEM_LIMIT = 56 * 1024 * 1024

TOKEN_TILE = 512
Q_TILE_MLA = 1024
Q_TILE_GQA = 512
KV_CHUNK = 1024
FF_CHUNK = 256
N_FF_CHUNKS = D_FF // FF_CHUNK

MAX_SHIFT_FREE_SCORE = 60.0

_OFF_CQ = 0
_OFF_CKV = _OFF_CQ + MLA_Q_LORA
_OFF_KROPE = _OFF_CKV + MLA_KV_LORA
_OFF_GQ = _OFF_KROPE + MLA_ROPE
_OFF_GK = _OFF_GQ + GQA_HEADS * GQA_HEAD_DIM
_OFF_GV = _OFF_GK + GQA_KV_HEADS * GQA_HEAD_DIM
_OFF_POOL = _OFF_GV + GQA_KV_HEADS * GQA_HEAD_DIM
_OFF_GATE = _OFF_POOL + POOL_WIDTH


def _mla_head_perm():
    perm = np.full((LANES,), -1, np.int64)
    half = MLA_ROPE // 2
    perm[0:half] = MLA_NOPE + np.arange(half)
    perm[ROPE_LANE_SHIFT:ROPE_LANE_SHIFT + half] = MLA_NOPE + half + np.arange(half)
    perm[half:ROPE_LANE_SHIFT] = np.arange(ROPE_LANE_SHIFT - half)
    rest = MLA_NOPE - (ROPE_LANE_SHIFT - half)
    perm[ROPE_LANE_SHIFT + half:ROPE_LANE_SHIFT + half + rest] = (ROPE_LANE_SHIFT - half) + np.arange(rest)
    return perm


def _gqa_head_perm():
    perm = np.full((LANES,), -1, np.int64)
    half = GQA_HEAD_DIM // 2
    perm[0:half] = np.arange(half)
    perm[ROPE_LANE_SHIFT:ROPE_LANE_SHIFT + half] = half + np.arange(half)
    return perm


def _take_cols(w, idx):
    idx = np.asarray(idx)
    cols = jnp.take(w, jnp.asarray(np.maximum(idx, 0)), axis=-1)
    return jnp.where(jnp.asarray(idx >= 0), cols, jnp.zeros((), w.dtype))


def _prep_in_cols():
    mla = _mla_head_perm()
    gqa = _gqa_head_perm()
    cols = [np.arange(_OFF_CQ, _OFF_CQ + MLA_Q_LORA), np.arange(_OFF_CKV, _OFF_CKV + MLA_KV_LORA)]
    kr = np.where(mla >= MLA_NOPE, _OFF_KROPE + (mla - MLA_NOPE), -1)
    cols.append(kr)
    for h in range(GQA_HEADS):
        cols.append(np.where(gqa >= 0, _OFF_GQ + h * GQA_HEAD_DIM + gqa, -1))
    for g in range(GQA_KV_HEADS):
        cols.append(np.where(gqa >= 0, _OFF_GK + g * GQA_HEAD_DIM + gqa, -1))
    for g in range(GQA_KV_HEADS):
        v = np.full((LANES,), -1, np.int64)
        v[:GQA_HEAD_DIM] = _OFF_GV + g * GQA_HEAD_DIM + np.arange(GQA_HEAD_DIM)
        cols.append(v)
    return np.concatenate(cols)


_PREP_W = MLA_Q_LORA + MLA_KV_LORA + LANES + (GQA_HEADS + 2 * GQA_KV_HEADS) * LANES
_P_CQ = 0
_P_CKV = MLA_Q_LORA
_P_KROPE = _P_CKV + MLA_KV_LORA
_P_GQ = _P_KROPE + LANES
_P_GK = _P_GQ + GQA_HEADS * LANES
_P_GV = _P_GK + GQA_KV_HEADS * LANES


def _uq_cols():
    mla = _mla_head_perm()
    return np.concatenate([np.where(mla >= 0, h * MLA_QK + mla, -1) for h in range(MLA_HEADS)])


def _ukv_cols():
    mla = _mla_head_perm()
    per_head = MLA_NOPE + MLA_V
    k = [np.where((mla >= 0) & (mla < MLA_NOPE), h * per_head + mla, -1) for h in range(MLA_HEADS)]
    v = []
    for h in range(MLA_HEADS):
        vh = np.full((LANES,), -1, np.int64)
        vh[:MLA_V] = h * per_head + MLA_NOPE + np.arange(MLA_V)
        v.append(vh)
    return np.concatenate(k + v)


def _rope_tables(seq_len, rot_dim, perm, first_rope_dim):
    rows = seq_len // GRID_W
    row_idx = jnp.repeat(jnp.arange(rows, dtype=F32), GRID_W)
    col_idx = jnp.tile(jnp.arange(GRID_W, dtype=F32), rows)
    n_axis = rot_dim // 4
    inv_freq = ROPE_THETA ** (-jnp.arange(n_axis, dtype=F32) / n_axis)
    ang = jnp.concatenate([row_idx[:, None] * inv_freq, col_idx[:, None] * inv_freq], axis=-1)
    cos, sin = jnp.cos(ang), jnp.sin(ang)
    half = rot_dim // 2
    rel = perm - first_rope_dim
    is_rope = (perm >= first_rope_dim)
    freq = np.where(is_rope, rel % half, 0)
    sign = np.where(is_rope, np.where(rel < half, -1.0, 1.0), 0.0).astype(np.float32)
    c = jnp.where(jnp.asarray(is_rope), jnp.take(cos, jnp.asarray(freq), axis=1), 1.0)
    s = jnp.take(sin, jnp.asarray(freq), axis=1) * jnp.asarray(sign)
    return c.astype(F32), s.astype(F32)


def _rms(x, g):
    return x * lax.rsqrt(jnp.mean(x * x, axis=-1, keepdims=True) + EPS) * g


def _dot(a, b):
    return jnp.dot(a, b, preferred_element_type=F32)


def _per_head_ones():
    head = lambda axis: lax.broadcasted_iota(jnp.int32, (2 * LANES, 2 * LANES), axis) // LANES
    return jnp.where(head(0) == head(1), 1.0, 0.0).astype(BF16)


def _head_pair_norm_rope(xp, g, cos, sin, dim, ones_bd):
    sq = xp * xp
    hi = sq.astype(BF16)
    lo = (sq - hi.astype(F32)).astype(BF16)
    ms = (_dot(hi, ones_bd) + _dot(lo, ones_bd)) * (1.0 / dim)
    outs = []
    for i in range(2):
        sl = slice(i * LANES, (i + 1) * LANES)
        xn = xp[:, sl] * lax.rsqrt(ms[:, sl] + EPS) * g
        outs.append((xn * cos + pltpu.roll(xn, ROPE_LANE_SHIFT, 1) * sin).astype(BF16))
    return jnp.concatenate(outs, axis=1)


def _with_denominator_column(v):
    lane = lax.broadcasted_iota(jnp.int32, v.shape, 1)
    return jnp.where((lane & (LANES - 1)) == DENOM_LANE, 1.0, v)


def _store_values_transposed(vt_ref, v):
    for hd in range(vt_ref.shape[1]):
        vt_ref[0, hd, 0] = v[:, hd * LANES:(hd + 1) * LANES].T.astype(BF16)


def _prep_body(x_ref, g_ref, win_ref, qng_ref, wuq_ref, kvng_ref, wukv_ref, qhg_ref, khg_ref,
               gqg_ref, gkg_ref, cm_ref, sm_ref, cg_ref, sg_ref,
               qm_ref, km_ref, vtm_ref, qg_ref, kg_ref, vtg_ref):
    h = _rms(x_ref[0], g_ref[...]).astype(BF16)
    proj = _dot(h, win_ref[...])
    cq = _rms(proj[:, _P_CQ:_P_CQ + MLA_Q_LORA], qng_ref[...]).astype(BF16)
    ckv = _rms(proj[:, _P_CKV:_P_CKV + MLA_KV_LORA], kvng_ref[...]).astype(BF16)
    krope = proj[:, _P_KROPE:_P_KROPE + LANES]
    q = _dot(cq, wuq_ref[...])
    kv = _dot(ckv, wukv_ref[...])
    cm, sm, cg, sg = cm_ref[...], sm_ref[...], cg_ref[...], sg_ref[...]
    qhg, khg, gqg, gkg = qhg_ref[...], khg_ref[...], gqg_ref[...], gkg_ref[...]
    ones_bd = _per_head_ones()
    krope2 = jnp.concatenate([krope, krope], axis=1)
    for pr in range(MLA_HEADS // 2):
        sl = slice(2 * pr * LANES, 2 * (pr + 1) * LANES)
        qm_ref[0, :, sl] = _head_pair_norm_rope(q[:, sl], qhg, cm, sm, MLA_QK, ones_bd)
        km_ref[0, :, sl] = _head_pair_norm_rope(kv[:, sl] + krope2, khg, cm, sm, MLA_QK, ones_bd)
    _store_values_transposed(vtm_ref, _with_denominator_column(kv[:, MLA_HEADS * LANES:]))
    for pr in range(GQA_HEADS // 2):
        sl = slice(2 * pr * LANES, 2 * (pr + 1) * LANES)
        src = proj[:, _P_GQ + sl.start:_P_GQ + sl.stop]
        qg_ref[0, :, sl] = _head_pair_norm_rope(src, gqg, cg, sg, GQA_HEAD_DIM, ones_bd)
    for pr in range(GQA_KV_HEADS // 2):
        sl = slice(2 * pr * LANES, 2 * (pr + 1) * LANES)
        src = proj[:, _P_GK + sl.start:_P_GK + sl.stop]
        kg_ref[0, :, sl] = _head_pair_norm_rope(src, gkg, cg, sg, GQA_HEAD_DIM, ones_bd)
    _store_values_transposed(vtg_ref, _with_denominator_column(proj[:, _P_GV:_P_GV + GQA_KV_HEADS * LANES]))


def _const_spec(shape):
    nd = len(shape)
    return pl.BlockSpec(shape, lambda *_: (0,) * nd)


def _prep(x, g, win, qng, wuq, kvng, wukv, qhg, khg, gqg, gkg, cm, sm, cg, sg):
    b, s, d = x.shape
    t = TOKEN_TILE
    tok = lambda w: pl.BlockSpec((1, t, w), lambda bi, i: (bi, i, 0))
    tab = pl.BlockSpec((t, LANES), lambda bi, i: (i, 0))
    consts = (g, win, qng, wuq, kvng, wukv, qhg, khg, gqg, gkg)
    per = KV_CHUNK // t
    vt = lambda heads: (pl.BlockSpec((1, heads, 1, LANES, t), lambda bi, i: (bi, 0, i // per, 0, i % per)),
                        jax.ShapeDtypeStruct((b, heads, s // KV_CHUNK, LANES, KV_CHUNK), BF16))
    act = lambda w: (tok(w), jax.ShapeDtypeStruct((b, s, w), BF16))
    outs = (act(MLA_HEADS * LANES), act(MLA_HEADS * LANES), vt(MLA_HEADS),
            act(GQA_HEADS * LANES), act(GQA_KV_HEADS * LANES), vt(GQA_KV_HEADS))
    return pl.pallas_call(
        _prep_body,
        grid=(b, s // t),
        in_specs=[tok(d)] + [_const_spec(c.shape) for c in consts] + [tab] * 4,
        out_specs=[spec for spec, _ in outs],
        out_shape=[shape for _, shape in outs],
        compiler_params=pltpu.CompilerParams(
            dimension_semantics=("parallel", "parallel"), vmem_limit_bytes=VMEM_LIMIT),
        name="prep",
    )(x, *consts, cm, sm, cg, sg)


def _flash(q, k_ref, k_off, vt_ref, seq_len, exp2_scale, track_max):
    rows = q.shape[0]
    nt_dims = (((1,), (1,)), ((), ()))

    def chunk(j):
        start = pl.multiple_of(j * KV_CHUNK, KV_CHUNK)
        kc = k_ref[0, pl.ds(start, KV_CHUNK), k_off:k_off + LANES]
        return lax.dot_general(kc, q, nt_dims, preferred_element_type=F32), vt_ref[j]

    def step_plain(j, acc):
        st, vt = chunk(j)
        return acc + _dot(vt, jnp.exp2(st * exp2_scale).astype(BF16))

    def step_max(j, carry):
        m, acc = carry
        st, vt = chunk(j)
        m_new = jnp.maximum(m, jnp.max(st, axis=0, keepdims=True))
        alpha = jnp.exp2((m - m_new) * exp2_scale)
        p = jnp.exp2((st - m_new) * exp2_scale)
        return m_new, alpha * acc + _dot(vt, p.astype(BF16))

    acc0 = jnp.zeros((LANES, rows), F32)
    n_chunks = seq_len // KV_CHUNK
    if track_max:
        _, acc = lax.fori_loop(0, n_chunks, step_max, (jnp.full((1, rows), -1e30, F32), acc0))
    else:
        acc = lax.fori_loop(0, n_chunks, step_plain, acc0, unroll=True)
    return (acc / acc[DENOM_LANE:DENOM_LANE + 1, :]).T


def _pair_heads(o_even, o_odd):
    lane = lax.broadcasted_iota(jnp.int32, o_even.shape, 1)
    return jnp.where(lane < HEAD_V, o_even, pltpu.roll(o_odd, HEAD_V, 1))


def _attn_mla_body(q_ref, k_ref, vt_ref, o_ref, *, seq_len, exp2_scale, track_max):
    o0 = _flash(q_ref[0, :, 0:LANES], k_ref, 0, vt_ref.at[0, 0], seq_len, exp2_scale, track_max)
    o1 = _flash(q_ref[0, :, LANES:2 * LANES], k_ref, LANES, vt_ref.at[0, 1], seq_len, exp2_scale, track_max)
    o_ref[0] = _pair_heads(o0, o1).astype(o_ref.dtype)


def _attn_gqa_body(q_ref, k_ref, vt_ref, o_ref, *, seq_len, exp2_scale, track_max):
    rep = GQA_HEADS // GQA_KV_HEADS
    tq = q_ref.shape[1]
    q = jnp.concatenate([q_ref[0, :, i * LANES:(i + 1) * LANES] for i in range(rep)], axis=0)
    o = _flash(q, k_ref, 0, vt_ref.at[0, 0], seq_len, exp2_scale, track_max)
    for j in range(rep // 2):
        pair = _pair_heads(o[(2 * j) * tq:(2 * j + 1) * tq], o[(2 * j + 1) * tq:(2 * j + 2) * tq])
        o_ref[0, :, j * LANES:(j + 1) * LANES] = pair.astype(o_ref.dtype)


def _vt_spec(heads_per_step, n_chunks):
    return pl.BlockSpec((1, heads_per_step, n_chunks, LANES, KV_CHUNK), lambda bi, h, qi: (bi, h, 0, 0, 0))


def _attn_mla(q, k, v, *, track_max):
    b, s, _ = q.shape
    tq = Q_TILE_MLA
    body = functools.partial(_attn_mla_body, seq_len=s, exp2_scale=MLA_QK ** -0.5 * math.log2(math.e),
                             track_max=track_max)
    return pl.pallas_call(
        body,
        grid=(b, MLA_HEADS // 2, s // tq),
        in_specs=[pl.BlockSpec((1, tq, 2 * LANES), lambda bi, hp, qi: (bi, qi, hp)),
                  pl.BlockSpec((1, s, 2 * LANES), lambda bi, hp, qi: (bi, 0, hp)),
                  _vt_spec(2, s // KV_CHUNK)],
        out_specs=pl.BlockSpec((1, tq, LANES), lambda bi, hp, qi: (bi, qi, hp)),
        out_shape=jax.ShapeDtypeStruct((b, s, MLA_HEADS * MLA_V), BF16),
        compiler_params=pltpu.CompilerParams(
            dimension_semantics=("parallel", "parallel", "arbitrary"), vmem_limit_bytes=VMEM_LIMIT),
        name="attn_mla_max" if track_max else "attn_mla",
    )(q, k, v)


def _attention(attn, q, k, v, head_dim, q_gain, k_gain):
    bound = math.sqrt(head_dim) * jnp.max(jnp.abs(q_gain)) * jnp.max(jnp.abs(k_gain))
    return lax.cond(bound <= MAX_SHIFT_FREE_SCORE,
                    functools.partial(attn, track_max=False), functools.partial(attn, track_max=True), q, k, v)


def _attn_gqa(q, k, v, *, track_max):
    b, s, _ = q.shape
    tq = Q_TILE_GQA
    rep = GQA_HEADS // GQA_KV_HEADS
    body = functools.partial(_attn_gqa_body, seq_len=s, exp2_scale=GQA_HEAD_DIM ** -0.5 * math.log2(math.e),
                             track_max=track_max)
    return pl.pallas_call(
        body,
        grid=(b, GQA_KV_HEADS, s // tq),
        in_specs=[pl.BlockSpec((1, tq, rep * LANES), lambda bi, g, qi: (bi, qi, g)),
                  pl.BlockSpec((1, s, LANES), lambda bi, g, qi: (bi, 0, g)),
                  _vt_spec(1, s // KV_CHUNK)],
        out_specs=pl.BlockSpec((1, tq, rep * GQA_HEAD_DIM), lambda bi, g, qi: (bi, qi, g)),
        out_shape=jax.ShapeDtypeStruct((b, s, GQA_HEADS * GQA_HEAD_DIM), BF16),
        compiler_params=pltpu.CompilerParams(
            dimension_semantics=("parallel", "parallel", "arbitrary"), vmem_limit_bytes=VMEM_LIMIT),
        name="attn_gqa_max" if track_max else "attn_gqa",
    )(q, k, v)


def _fill_normed_ext(h_scr, x, xp_ref, xn_ref, g):
    i = pl.program_id(1)
    t = x.shape[0]
    h_scr[HALO:HALO + t] = _rms(x, g).astype(BF16)
    hp = _rms(xp_ref[0], g)
    hn = _rms(xn_ref[0], g)
    h_scr[0:HALO] = jnp.where(i > 0, hp, 0.0).astype(BF16)
    h_scr[HALO + t:] = jnp.where(i < pl.num_programs(1) - 1, hn, 0.0).astype(BF16)


def _halo_specs(t, d):
    per = t // HALO
    main = pl.BlockSpec((1, t, d), lambda bi, i: (bi, i, 0))
    prev = pl.BlockSpec((1, HALO, d), lambda bi, i: (bi, jnp.maximum(i * per - 1, 0), 0))
    nxt = lambda n_blocks: pl.BlockSpec(
        (1, HALO, d), lambda bi, i: (bi, jnp.minimum((i + 1) * per, n_blocks - 1), 0))
    return main, prev, nxt


def _merge_body(x_ref, xp_ref, xn_ref, om_ref, og_ref, g_ref, wgate_ref, wpool_ref, poolw_ref, pscale_ref,
                wbm_ref, wbg_ref, wbp_ref, wout_ref, o_ref, h_scr, p_scr, *, seq_len):
    x = x_ref[0]
    t = x.shape[0]
    _fill_normed_ext(h_scr, x, xp_ref, xn_ref, g_ref[...])
    h = h_scr[HALO:HALO + t]
    p_scr[...] = _dot(h_scr[...], wpool_ref[...])
    pos = pl.program_id(1) * t + lax.broadcasted_iota(jnp.int32, (t, 1), 0)
    ys = []
    for gi, w in enumerate(POOL_WINDOWS):
        sl = slice(gi * POOL_GROUP, (gi + 1) * POOL_GROUP)
        tot = None
        for d in range(-(w // 2), w - w // 2):
            part = p_scr[HALO + d:HALO + d + t, sl]
            tot = part if tot is None else tot + part
        cnt = jnp.minimum(pos + (w - w // 2), seq_len) - jnp.maximum(pos - w // 2, 0)
        mixed = tot / cnt.astype(F32) - p_scr[HALO:HALO + t, sl]
        ys.append(_dot(mixed.astype(BF16), poolw_ref[gi]))
    y = (jnp.concatenate(ys, axis=1) * pscale_ref[...]).astype(BF16)
    gates = jax.nn.sigmoid(_dot(h, wgate_ref[...]))
    merged = (gates[:, 0:D_MODEL] * _dot(om_ref[0], wbm_ref[...])
              + gates[:, D_MODEL:2 * D_MODEL] * _dot(og_ref[0], wbg_ref[...])
              + gates[:, 2 * D_MODEL:3 * D_MODEL] * _dot(y, wbp_ref[...]))
    o_ref[0] = x + _dot(merged.astype(BF16), wout_ref[...])


def _merge(x, om, og, g, wgate, wpool, poolw, pscale, wbm, wbg, wbp, wout):
    b, s, d = x.shape
    t = TOKEN_TILE
    main, prev, nxt = _halo_specs(t, d)
    tok = lambda w: pl.BlockSpec((1, t, w), lambda bi, i: (bi, i, 0))
    consts = (g, wgate, wpool, poolw, pscale, wbm, wbg, wbp, wout)
    return pl.pallas_call(
        functools.partial(_merge_body, seq_len=s),
        grid=(b, s // t),
        in_specs=[main, prev, nxt(s // HALO), tok(om.shape[-1]), tok(og.shape[-1])]
                 + [_const_spec(c.shape) for c in consts],
        out_specs=main,
        out_shape=jax.ShapeDtypeStruct(x.shape, x.dtype),
        scratch_shapes=[pltpu.VMEM((t + 2 * HALO, d), BF16), pltpu.VMEM((t + 2 * HALO, POOL_WIDTH), F32)],
        compiler_params=pltpu.CompilerParams(
            dimension_semantics=("parallel", "parallel"), vmem_limit_bytes=VMEM_LIMIT),
        name="merge",
    )(x, x, x, om, og, *consts)


def _ffn_body(x_ref, xp_ref, xn_ref, g_ref, wup_ref, cw_ref, wdn_ref, o_ref, h_scr, u_scr, act_scr):
    x = x_ref[0]
    t = x.shape[0]
    _fill_normed_ext(h_scr, x, xp_ref, xn_ref, g_ref[...])

    def conv_half(k, u):
        u[...] = _dot(h_scr[...], wup_ref[k])
        cw = cw_ref[k]
        return (cw[0:1] * u[HALO - 1:HALO - 1 + t] + cw[1:2] * u[HALO:HALO + t]
                + cw[2:3] * u[HALO + 1:HALO + 1 + t] + cw[3:4])

    for c in range(N_FF_CHUNKS):
        u_gate = conv_half(c, u_scr.at[2 * (c % 2)])
        u_val = conv_half(c + N_FF_CHUNKS, u_scr.at[2 * (c % 2) + 1])
        act_scr[:, c * FF_CHUNK:(c + 1) * FF_CHUNK] = (jax.nn.silu(u_gate) * u_val).astype(BF16)
    o_ref[0] = x + _dot(act_scr[...], wdn_ref[...])


def _ffn(x, g, wup, cw, wdn):
    b, s, d = x.shape
    t = TOKEN_TILE
    main, prev, nxt = _halo_specs(t, d)
    consts = (g, wup, cw, wdn)
    return pl.pallas_call(
        _ffn_body,
        grid=(b, s // t),
        in_specs=[main, prev, nxt(s // HALO)] + [_const_spec(c.shape) for c in consts],
        out_specs=main,
        out_shape=jax.ShapeDtypeStruct(x.shape, x.dtype),
        scratch_shapes=[pltpu.VMEM((t + 2 * HALO, d), BF16), pltpu.VMEM((4, t + 2 * HALO, FF_CHUNK), F32),
                        pltpu.VMEM((t, D_FF), BF16)],
        compiler_params=pltpu.CompilerParams(
            dimension_semantics=("parallel", "parallel"), vmem_limit_bytes=VMEM_LIMIT),
        name="ffn",
    )(x, x, x, *consts)


def _pad_gain(g, perm):
    return _take_cols(g[None, :], perm)


def kernel(x, attn_norm_g, w_in, mla_q_norm_g, mla_w_uq, mla_kv_norm_g, mla_w_ukv, mla_q_head_g, mla_k_head_g, gqa_q_head_g, gqa_k_head_g, pool_w, pool_scale, w_branch_mla, w_branch_gqa, w_branch_pool, w_out, ffn_norm_g, ffn_w_up, ffn_conv_w, ffn_conv_b, ffn_w_down):
    b, s, d = x.shape
    assert d == D_MODEL and s % GRID_W == 0
    assert all(s % t == 0 for t in (TOKEN_TILE, Q_TILE_MLA, Q_TILE_GQA, KV_CHUNK)), s
    assert KV_CHUNK % TOKEN_TILE == 0
    depth = w_in.shape[0]
    mla_perm, gqa_perm = _mla_head_perm(), _gqa_head_perm()
    cm, sm = _rope_tables(s, MLA_ROPE, mla_perm, MLA_NOPE)
    cg, sg = _rope_tables(s, GQA_HEAD_DIM, gqa_perm, 0)
    prep_cols, uq_cols, ukv_cols = _prep_in_cols(), _uq_cols(), _ukv_cols()
    row = lambda v: v[None, :].astype(F32)
    for l in range(depth):
        qm, km, vtm, qg, kg, vtg = _prep(
            x, row(attn_norm_g[l]), _take_cols(w_in[l], prep_cols).astype(BF16),
            row(mla_q_norm_g[l]), _take_cols(mla_w_uq[l], uq_cols).astype(BF16),
            row(mla_kv_norm_g[l]), _take_cols(mla_w_ukv[l], ukv_cols).astype(BF16),
            _pad_gain(mla_q_head_g[l], mla_perm), _pad_gain(mla_k_head_g[l], mla_perm),
            _pad_gain(gqa_q_head_g[l], gqa_perm), _pad_gain(gqa_k_head_g[l], gqa_perm),
            cm, sm, cg, sg)
        o_mla = _attention(_attn_mla, qm, km, vtm, MLA_QK, mla_q_head_g[l], mla_k_head_g[l])
        o_gqa = _attention(_attn_gqa, qg, kg, vtg, GQA_HEAD_DIM, gqa_q_head_g[l], gqa_k_head_g[l])
        x = _merge(
            x, o_mla, o_gqa, row(attn_norm_g[l]),
            w_in[l][:, _OFF_GATE:].astype(BF16), w_in[l][:, _OFF_POOL:_OFF_GATE].astype(BF16),
            pool_w[l].astype(BF16), row(pool_scale[l]),
            w_branch_mla[l].astype(BF16), w_branch_gqa[l].astype(BF16), w_branch_pool[l].astype(BF16),
            w_out[l].astype(BF16))
        wup = ffn_w_up[l].astype(BF16).reshape(d, 2 * N_FF_CHUNKS, FF_CHUNK).transpose(1, 0, 2)
        cw = jnp.concatenate([ffn_conv_w[l], ffn_conv_b[l][None, :], jnp.zeros((4, 2 * D_FF), F32)], axis=0)
        cw = cw.reshape(8, 2 * N_FF_CHUNKS, FF_CHUNK).transpose(1, 0, 2)
        wdn = ffn_w_down[l].astype(BF16)
        x = _ffn(x, row(ffn_norm_g[l]), wup, cw, wdn)
    return x
```
